```python
import jax, jax.numpy as jnp
from jax import lax
import numpy as np

D_MODEL = 1024
BATCH = 8
SEQ = 8192
DEPTH = 2
DEC_BATCH = 16
DEC_SEQ = 32
PAST_LEN = 1024

CHUNK = 64
N_HEADS = 16
N_KV_HEADS = 2
HEAD_DIM = 64
GROUP = N_HEADS // N_KV_HEADS
Q_DIM = N_HEADS * HEAD_DIM
KV_DIM = N_KV_HEADS * HEAD_DIM
WINDOW = 128
WIN_CHUNKS = WINDOW // CHUNK
SPAN = (WIN_CHUNKS + 1) * CHUNK
D_CONV = D_MODEL
CONV_WIDTH = 3
D_FF = 2816
EPS = 1e-6
ATTN_SCALE = HEAD_DIM ** -0.5
SPLITS = (Q_DIM, KV_DIM, KV_DIM, D_CONV, D_CONV, D_CONV, D_MODEL, D_MODEL)
N_IN = sum(SPLITS)
SPLIT_AT = tuple(sum(SPLITS[:i + 1]) for i in range(len(SPLITS) - 1))

kernel_name = "hybrid_swa_sink_shortconv_macaron_step"


def rms_norm(x, g):
    xf = x.astype(jnp.float32)
    y = xf * lax.rsqrt(jnp.mean(xf * xf, axis=-1, keepdims=True) + EPS)
    return (y * g.astype(jnp.float32)).astype(x.dtype)


def half_ffn(x, g, w_gate, w_up, w_down):
    h = rms_norm(x, g)
    return x + 0.5 * ((jax.nn.silu(h @ w_gate) * (h @ w_up)) @ w_down)


def mixer_inputs(x, g, w_in, b_in):
    h = rms_norm(x, g)
    return jnp.split(h @ w_in + b_in, SPLIT_AT, axis=-1)


def sink_softmax(s, sink):
    m = jnp.maximum(jnp.max(s, axis=-1, keepdims=True), sink)
    e = jnp.exp(s - m)
    return e / (jnp.sum(e, axis=-1, keepdims=True) + jnp.exp(sink - m))


def banded_sink_attention(q, k, v, sinks):
    b, n = q.shape[:2]
    nc = n // CHUNK
    qc = q.reshape(b, nc, CHUNK, N_KV_HEADS, GROUP, HEAD_DIM)

    def windows(t):
        tp = jnp.pad(t.reshape(b, n, N_KV_HEADS, HEAD_DIM),
                     ((0, 0), (WIN_CHUNKS * CHUNK, 0), (0, 0), (0, 0)))
        tp = tp.reshape(b, nc + WIN_CHUNKS, CHUNK, N_KV_HEADS, HEAD_DIM)
        return jnp.concatenate([tp[:, i:i + nc] for i in range(WIN_CHUNKS + 1)], axis=2)

    kw, vw = windows(k), windows(v)
    s = jnp.einsum('bcqhgd,bcjhd->bchgqj', qc, kw,
                   preferred_element_type=jnp.float32) * ATTN_SCALE
    key_chunk = jnp.arange(nc)[:, None] - WIN_CHUNKS + jnp.arange(SPAN)[None, :] // CHUNK
    s = jnp.where((key_chunk >= 0)[None, :, None, None, None, :], s, -jnp.inf)
    sink = sinks.astype(jnp.float32).reshape(1, 1, N_KV_HEADS, GROUP, 1, 1)
    p = sink_softmax(s, sink).astype(vw.dtype)
    o = jnp.einsum('bchgqj,bcjhd->bcqhgd', p, vw)
    return o.reshape(b, n, Q_DIM)


def sample_sink_attention(q, k_all, v_all, sinks):
    b, n = q.shape[:2]
    qs = q.reshape(b, n, N_KV_HEADS, GROUP, HEAD_DIM)
    s = jnp.einsum('bqhgd,bjhd->bhgqj', qs, k_all,
                   preferred_element_type=jnp.float32) * ATTN_SCALE
    sink = sinks.astype(jnp.float32).reshape(1, N_KV_HEADS, GROUP, 1, 1)
    p = sink_softmax(s, sink).astype(v_all.dtype)
    o = jnp.einsum('bhgqj,bjhd->bqhgd', p, v_all)
    return o.reshape(b, n, Q_DIM)


def causal_dwconv(up, w, n):
    out = up[:, 0:n] * w[0]
    for j in range(1, CONV_WIDTH):
        out = out + up[:, j:j + n] * w[j]
    return out


def merge_branches(attn, conv, g_attn, g_conv, w_o_attn, w_o_conv, w_o):
    m = jax.nn.sigmoid(g_attn) * (attn @ w_o_attn) + jax.nn.sigmoid(g_conv) * (conv @ w_o_conv)
    return m @ w_o


def setup_inputs(seed: int = 0) -> dict:
    key = jax.random.key(seed)
    ks = jax.random.split(key, 24)

    def nrm(k, shape, scale):
        return jax.random.normal(k, shape, jnp.float32) * scale

    win_keep = min(WINDOW, PAST_LEN)
    return {
        "x_prompt": nrm(ks[0], (BATCH, SEQ, D_MODEL), 1.0),
        "x_sample": nrm(ks[1], (DEC_BATCH, DEC_SEQ, D_MODEL), 1.0),
        "cache_k": nrm(ks[2], (DEPTH, DEC_BATCH, win_keep, N_KV_HEADS, HEAD_DIM), 1.0),
        "cache_v": nrm(ks[3], (DEPTH, DEC_BATCH, win_keep, N_KV_HEADS, HEAD_DIM), 1.0),
        "state_conv": nrm(ks[4], (DEPTH, DEC_BATCH, CONV_WIDTH - 1, D_CONV), 1.0),
        "norm_ffn1": 1.0 + nrm(ks[5], (DEPTH, D_MODEL), 0.05),
        "w1_gate": nrm(ks[6], (DEPTH, D_MODEL, D_FF), D_MODEL ** -0.5),
        "w1_up": nrm(ks[7], (DEPTH, D_MODEL, D_FF), D_MODEL ** -0.5),
        "w1_down": nrm(ks[8], (DEPTH, D_FF, D_MODEL), D_FF ** -0.5),
        "norm_mix": 1.0 + nrm(ks[9], (DEPTH, D_MODEL), 0.05),
        "w_in": nrm(ks[10], (DEPTH, D_MODEL, N_IN), D_MODEL ** -0.5),
        "b_in": nrm(ks[11], (DEPTH, N_IN), 0.02),
        "sinks": nrm(ks[12], (DEPTH, N_HEADS), 0.5),
        "conv_w": nrm(ks[13], (DEPTH, CONV_WIDTH, D_CONV), CONV_WIDTH ** -0.5),
        "w_o_attn": nrm(ks[14], (DEPTH, Q_DIM, D_MODEL), Q_DIM ** -0.5),
        "w_o_conv": nrm(ks[15], (DEPTH, D_CONV, D_MODEL), D_CONV ** -0.5),
        "w_o": nrm(ks[16], (DEPTH, D_MODEL, D_MODEL), D_MODEL ** -0.5),
        "norm_ffn2": 1.0 + nrm(ks[17], (DEPTH, D_MODEL), 0.05),
        "w2_gate": nrm(ks[18], (DEPTH, D_MODEL, D_FF), D_MODEL ** -0.5),
        "w2_up": nrm(ks[19], (DEPTH, D_MODEL, D_FF), D_MODEL ** -0.5),
        "w2_down": nrm(ks[20], (DEPTH, D_FF, D_MODEL), D_FF ** -0.5),
        "norm_final": 1.0 + nrm(ks[21], (D_MODEL,), 0.05),
    }


def reference(x_prompt, x_sample, cache_k, cache_v, state_conv,
              norm_ffn1, w1_gate, w1_up, w1_down,
              norm_mix, w_in, b_in, sinks, conv_w, w_o_attn, w_o_conv, w_o,
              norm_ffn2, w2_gate, w2_up, w2_down, norm_final):
    xp, xs = x_prompt, x_sample
    n_p, n_s = xp.shape[1], xs.shape[1]
    keep_s = cache_k.shape[2]
    keep_p = min(WINDOW, n_p)
    kp_l, vp_l, cp_l, ks_l, vs_l, cs_l = [], [], [], [], [], []
    for l in range(DEPTH):
        xp = half_ffn(xp, norm_ffn1[l], w1_gate[l], w1_up[l], w1_down[l])
        xs = half_ffn(xs, norm_ffn1[l], w1_gate[l], w1_up[l], w1_down[l])

        q, k, v, b_gate, c_gate, h_conv, g_attn, g_conv = mixer_inputs(xp, norm_mix[l], w_in[l], b_in[l])
        attn = banded_sink_attention(q, k, v, sinks[l])
        u = c_gate * h_conv
        up = jnp.pad(u, ((0, 0), (CONV_WIDTH - 1, 0), (0, 0)))
        conv = b_gate * causal_dwconv(up, conv_w[l], n_p)
        xp = xp + merge_branches(attn, conv, g_attn, g_conv, w_o_attn[l], w_o_conv[l], w_o[l])
        bp = xp.shape[0]
        kp_l.append(k[:, n_p - keep_p:].reshape(bp, keep_p, N_KV_HEADS, HEAD_DIM))
        vp_l.append(v[:, n_p - keep_p:].reshape(bp, keep_p, N_KV_HEADS, HEAD_DIM))
        cp_l.append(u[:, n_p - (CONV_WIDTH - 1):])

        q, k, v, b_gate, c_gate, h_conv, g_attn, g_conv = mixer_inputs(xs, norm_mix[l], w_in[l], b_in[l])
        bs = xs.shape[0]
        k_all = jnp.concatenate([cache_k[l], k.reshape(bs, n_s, N_KV_HEADS, HEAD_DIM)], axis=1)
        v_all = jnp.concatenate([cache_v[l], v.reshape(bs, n_s, N_KV_HEADS, HEAD_DIM)], axis=1)
        attn = sample_sink_attention(q, k_all, v_all, sinks[l])
        u = c_gate * h_conv
        up = jnp.concatenate([state_conv[l], u], axis=1)
        conv = b_gate * causal_dwconv(up, conv_w[l], n_s)
        xs = xs + merge_branches(attn, conv, g_attn, g_conv, w_o_attn[l], w_o_conv[l], w_o[l])
        ks_l.append(k_all[:, -keep_s:])
        vs_l.append(v_all[:, -keep_s:])
        cs_l.append(up[:, -(CONV_WIDTH - 1):])

        xp = half_ffn(xp, norm_ffn2[l], w2_gate[l], w2_up[l], w2_down[l])
        xs = half_ffn(xs, norm_ffn2[l], w2_gate[l], w2_up[l], w2_down[l])

    y_prompt = rms_norm(xp, norm_final)
    y_sample = rms_norm(xs, norm_final)
    return (y_prompt, y_sample, jnp.stack(kp_l), jnp.stack(vp_l), jnp.stack(cp_l),
            jnp.stack(ks_l), jnp.stack(vs_l), jnp.stack(cs_l))
```

```python
import functools

import jax
import jax.numpy as jnp
from jax import lax
from jax.experimental import pallas as pl
from jax.experimental.pallas import tpu as pltpu

F32 = jnp.float32
BF16 = jnp.bfloat16

CHUNK = 64
N_HEADS = 16
N_KV_HEADS = 2
HEAD_DIM = 64
GROUP = N_HEADS // N_KV_HEADS
PAIRS = GROUP // 2
KV_DIM = N_KV_HEADS * HEAD_DIM
WINDOW = 128
CONV_WIDTH = 3
EPS = 1e-6
ATTN_SCALE = HEAD_DIM ** -0.5

KEY_SPAN = 256
LANES = 128
SUBLANES = 8
VMEM_LIMIT_BYTES = 56 * 1024 * 1024


def _rms(x, g):
    y = x * lax.rsqrt(jnp.mean(x * x, axis=-1, keepdims=True) + EPS)
    return y * g


def _dot(a, b):
    return jnp.dot(a, b, preferred_element_type=F32)


def _dot_nt(a, b):
    return lax.dot_general(a, b, (((1,), (1,)), ((), ())), preferred_element_type=F32)


def _const_spec(shape):
    nd = len(shape)
    return pl.BlockSpec(shape, lambda *_: (0,) * nd, pipeline_mode=pl.Buffered(1))


def _ffn_body(x_ref, g_ref, wg_ref, wu_ref, wd_ref, gf_ref, o_ref, *, final_norm):
    x = x_ref[...]
    h = _rms(x, g_ref[...]).astype(BF16)
    gate = _dot(h, wg_ref[...])
    up = _dot(h, wu_ref[...])
    act = (gate * jax.nn.sigmoid(gate) * up).astype(BF16)
    out = x + 0.5 * _dot(act, wd_ref[...])
    if final_norm:
        out = _rms(out, gf_ref[...])
    o_ref[...] = out


def _ffn(x2, g, wg, wu, wd, gf, *, tile, final_norm):
    n, d = x2.shape
    dff = wg.shape[1]
    assert n % tile == 0
    return pl.pallas_call(
        functools.partial(_ffn_body, final_norm=final_norm),
        grid=(n // tile,),
        in_specs=[
            pl.BlockSpec((tile, d), lambda i: (i, 0)),
            _const_spec((1, d)),
            _const_spec((d, dff)),
            _const_spec((d, dff)),
            _const_spec((dff, d)),
            _const_spec((1, d)),
        ],
        out_specs=pl.BlockSpec((tile, d), lambda i: (i, 0)),
        out_shape=jax.ShapeDtypeStruct((n, d), F32),
        compiler_params=pltpu.CompilerParams(
            dimension_semantics=("arbitrary",), vmem_limit_bytes=VMEM_LIMIT_BYTES),
        name="ffn_final" if final_norm else "ffn",
    )(x2, g, wg, wu, wd, gf)


def _split_offsets(d):
    sizes = (N_HEADS * HEAD_DIM, KV_DIM, KV_DIM, d, d, d, d, d)
    offs = [0]
    for s in sizes:
        offs.append(offs[-1] + s)
    return offs


def _kv_placements(k2, v2):
    n = k2.shape[0]
    lo = lax.broadcasted_iota(jnp.int32, (n, LANES), 1) < HEAD_DIM
    k2r = pltpu.roll(k2, HEAD_DIM, 1)
    v2r = pltpu.roll(v2, HEAD_DIM, 1)
    ones_lo = jnp.where(lo, 1.0, 0.0)
    ones_hi = jnp.where(lo, 0.0, 1.0)

    def cat(a, b):
        return jnp.concatenate([a, b], axis=1).astype(BF16)

    ka = (jnp.where(lo, k2, 0.0).astype(BF16), jnp.where(lo, k2r, 0.0).astype(BF16))
    kb = (jnp.where(lo, 0.0, k2r).astype(BF16), jnp.where(lo, 0.0, k2).astype(BF16))
    va = (cat(jnp.where(lo, v2, 0.0), ones_lo), cat(jnp.where(lo, v2r, 0.0), ones_lo))
    vb = (cat(jnp.where(lo, 0.0, v2r), ones_hi), cat(jnp.where(lo, 0.0, v2), ones_hi))
    return ka, kb, va, vb


def _attend(qs, at, q_row0, nq, windows, sink_ref, bias):
    ka, kb, va, vb = windows
    qrows = pl.ds(q_row0, nq)
    lo = lax.broadcasted_iota(jnp.int32, (1, LANES), 1) < HEAD_DIM
    for h in range(N_KV_HEADS):
        base = h * GROUP * HEAD_DIM
        q_stack = jnp.concatenate(
            [qs[qrows, base + p * LANES:base + (p + 1) * LANES] for p in range(PAIRS)], axis=0)
        s0 = _dot_nt(q_stack, ka[h]) + bias
        s1 = _dot_nt(q_stack, kb[h]) + bias
        sink0 = sink_ref[h, 0]
        sink1 = sink_ref[h, 1]
        m0 = jnp.maximum(jnp.max(s0, axis=1, keepdims=True), sink0)
        m1 = jnp.maximum(jnp.max(s1, axis=1, keepdims=True), sink1)
        e0 = jnp.exp(s0 - m0).astype(BF16)
        e1 = jnp.exp(s1 - m1).astype(BF16)
        o2 = _dot(e0, va[h]) + _dot(e1, vb[h])
        e_sink = jnp.where(lo, jnp.exp(sink0 - m0), jnp.exp(sink1 - m1))
        o = (o2[:, 0:LANES] / (o2[:, LANES:2 * LANES] + e_sink)).astype(BF16)
        for p in range(PAIRS):
            at[qrows, base + p * LANES:base + (p + 1) * LANES] = o[p * nq:(p + 1) * nq, :]


def _conv_taps(ub, cw_ref, n):
    out = ub[pl.ds(SUBLANES - 2, n), :] * cw_ref[0:1, :]
    out = out + ub[pl.ds(SUBLANES - 1, n), :] * cw_ref[1:2, :]
    return out + ub[pl.ds(SUBLANES, n), :] * cw_ref[2:3, :]


CARRY = KEY_SPAN - CHUNK


def _mixer_prompt_body(x_ref, g_ref, win_ref, bin_ref, sink_ref, cw_ref, woa_ref, woc_ref, wo_ref,
                       xo_ref, kn_ref, vn_ref, un_ref,
                       ka, kb, va, vb, qs, at, ub, *, tile):
    d = x_ref.shape[1]
    offs = _split_offsets(d)
    t = pl.program_id(1)

    @pl.when(t == 0)
    def _():
        for r in (ka, kb, va, vb):
            r[:, 0:CARRY, :] = jnp.zeros((N_KV_HEADS, CARRY, r.shape[2]), BF16)
        ub[0:SUBLANES, :] = jnp.zeros((SUBLANES, d), F32)

    x = x_ref[...]
    h = _rms(x, g_ref[...]).astype(BF16)

    def proj(i):
        return _dot(h, win_ref[:, offs[i]:offs[i + 1]]) + bin_ref[:, offs[i]:offs[i + 1]]

    k2 = proj(1)
    v2 = proj(2)
    kn_ref[0] = k2[tile - WINDOW:, :]
    vn_ref[0] = v2[tile - WINDOW:, :]
    for ref, placed in zip((ka, kb, va, vb), _kv_placements(k2, v2)):
        for hh in range(N_KV_HEADS):
            ref[hh, CARRY:CARRY + tile, :] = placed[hh]
    qs[...] = (proj(0) * ATTN_SCALE).astype(BF16)

    def chunk_body(c, carry):
        r0 = pl.multiple_of(c * CHUNK, CHUNK)
        j = lax.broadcasted_iota(jnp.int32, (1, KEY_SPAN), 1)
        pos = t * tile + c * CHUNK - CARRY + j
        valid = jnp.logical_and(j >= KEY_SPAN - WINDOW - CHUNK, pos >= 0)
        bias = jnp.where(valid, 0.0, -jnp.inf).astype(F32)
        krows = pl.ds(r0, KEY_SPAN)
        windows = tuple(tuple(r[hh, krows, :] for hh in range(N_KV_HEADS))
                        for r in (ka, kb, va, vb))
        _attend(qs, at, r0, CHUNK, windows, sink_ref, bias)
        return carry

    lax.fori_loop(0, tile // CHUNK, chunk_body, 0)

    u = proj(4) * proj(5)
    ub[SUBLANES:SUBLANES + tile, :] = u
    conv = (proj(3) * _conv_taps(ub, cw_ref, tile)).astype(BF16)
    un_ref[0] = u[tile - SUBLANES:, :]
    ub[0:SUBLANES, :] = u[tile - SUBLANES:, :]

    merged = (jax.nn.sigmoid(proj(6)) * _dot(at[...], woa_ref[...])
              + jax.nn.sigmoid(proj(7)) * _dot(conv, woc_ref[...]))
    xo_ref[...] = x + _dot(merged.astype(BF16), wo_ref[...])

    for r in (ka, kb, va, vb):
        r[:, 0:CARRY, :] = r[:, tile:tile + CARRY, :]


def _mixer_prompt(x2, g, w_in, b_in, sink_cols, conv_w, woa, woc, wo, *, batch, tile):
    n, d = x2.shape
    seq = n // batch
    assert seq % tile == 0 and tile % CHUNK == 0 and tile >= CARRY and tile >= WINDOW
    nt = seq // tile
    n_in = w_in.shape[1]
    rows = CARRY + tile
    return pl.pallas_call(
        functools.partial(_mixer_prompt_body, tile=tile),
        grid=(batch, nt),
        in_specs=[
            pl.BlockSpec((tile, d), lambda b, t: (b * nt + t, 0)),
            _const_spec((1, d)),
            _const_spec((d, n_in)),
            _const_spec((1, n_in)),
            _const_spec(sink_cols.shape),
            _const_spec(conv_w.shape),
            _const_spec(woa.shape),
            _const_spec(woc.shape),
            _const_spec(wo.shape),
        ],
        out_specs=[
            pl.BlockSpec((tile, d), lambda b, t: (b * nt + t, 0)),
            pl.BlockSpec((1, WINDOW, KV_DIM), lambda b, t: (b, 0, 0)),
            pl.BlockSpec((1, WINDOW, KV_DIM), lambda b, t: (b, 0, 0)),
            pl.BlockSpec((1, SUBLANES, d), lambda b, t: (b, 0, 0)),
        ],
        out_shape=[
            jax.ShapeDtypeStruct((n, d), F32),
            jax.ShapeDtypeStruct((batch, WINDOW, KV_DIM), F32),
            jax.ShapeDtypeStruct((batch, WINDOW, KV_DIM), F32),
            jax.ShapeDtypeStruct((batch, SUBLANES, d), F32),
        ],
        scratch_shapes=[
            pltpu.VMEM((N_KV_HEADS, rows, LANES), BF16),
            pltpu.VMEM((N_KV_HEADS, rows, LANES), BF16),
            pltpu.VMEM((N_KV_HEADS, rows, 2 * LANES), BF16),
            pltpu.VMEM((N_KV_HEADS, rows, 2 * LANES), BF16),
            pltpu.VMEM((tile, d), BF16),
            pltpu.VMEM((tile, d), BF16),
            pltpu.VMEM((SUBLANES + tile, d), F32),
        ],
        compiler_params=pltpu.CompilerParams(
            dimension_semantics=("arbitrary", "arbitrary"), vmem_limit_bytes=VMEM_LIMIT_BYTES),
        name="mixer_prompt",
    )(x2, g, w_in, b_in, sink_cols, conv_w, woa, woc, wo)


def _mixer_sample_body(x_ref, ck_ref, cv_ref, st_ref, g_ref, win_ref, bin_ref, sink_ref, cw_ref,
                       woa_ref, woc_ref, wo_ref,
                       xo_ref, kn_ref, vn_ref, un_ref,
                       qs, at, cs, ub, *, n_streams, n_new):
    d = x_ref.shape[1]
    offs = _split_offsets(d)
    keep = ck_ref.shape[1]
    x = x_ref[...]
    h = _rms(x, g_ref[...]).astype(BF16)

    def proj(i):
        return _dot(h, win_ref[:, offs[i]:offs[i + 1]]) + bin_ref[:, offs[i]:offs[i + 1]]

    kn_all = proj(1)
    vn_all = proj(2)
    qs[...] = (proj(0) * ATTN_SCALE).astype(BF16)
    u_all = proj(4) * proj(5)

    pad0 = keep + n_new
    j = lax.broadcasted_iota(jnp.int32, (1, KEY_SPAN), 1)
    bias = jnp.where(j < pad0, 0.0, -jnp.inf).astype(F32)

    for b in range(n_streams):
        r0 = b * n_new
        k_new = kn_all[r0:r0 + n_new, :]
        v_new = vn_all[r0:r0 + n_new, :]
        kn_ref[b, 0:keep - n_new, :] = ck_ref[b, n_new:keep, :]
        kn_ref[b, keep - n_new:keep, :] = k_new
        vn_ref[b, 0:keep - n_new, :] = cv_ref[b, n_new:keep, :]
        vn_ref[b, keep - n_new:keep, :] = v_new
        old = _kv_placements(ck_ref[b], cv_ref[b])
        new = _kv_placements(k_new, v_new)
        windows = tuple(
            tuple(jnp.concatenate(
                [o[hh], w[hh], jnp.zeros((KEY_SPAN - pad0, o[hh].shape[1]), BF16)], axis=0)
                for hh in range(N_KV_HEADS))
            for o, w in zip(old, new))
        _attend(qs, at, r0, n_new, windows, sink_ref, bias)

        ub[b, 0:SUBLANES, :] = st_ref[b]
        ub[b, SUBLANES:SUBLANES + n_new, :] = u_all[r0:r0 + n_new, :]
        cs[r0:r0 + n_new, :] = _conv_taps(ub.at[b], cw_ref, n_new)
        un_ref[b] = u_all[r0 + n_new - SUBLANES:r0 + n_new, :]

    conv = (proj(3) * cs[...]).astype(BF16)
    merged = (jax.nn.sigmoid(proj(6)) * _dot(at[...], woa_ref[...])
              + jax.nn.sigmoid(proj(7)) * _dot(conv, woc_ref[...]))
    xo_ref[...] = x + _dot(merged.astype(BF16), wo_ref[...])


def _mixer_sample(x2, ck, cv, st, g, w_in, b_in, sink_cols, conv_w, woa, woc, wo):
    n, d = x2.shape
    n_streams, keep, _ = ck.shape
    n_new = n // n_streams
    assert keep + n_new <= KEY_SPAN and n_new % SUBLANES == 0 and keep >= n_new
    args = (x2, ck, cv, st, g, w_in, b_in, sink_cols, conv_w, woa, woc, wo)

    def whole(a):
        nd = a.ndim
        return pl.BlockSpec(a.shape, lambda i: (0,) * nd)

    out_shape = [
        jax.ShapeDtypeStruct((n, d), F32),
        jax.ShapeDtypeStruct((n_streams, keep, KV_DIM), F32),
        jax.ShapeDtypeStruct((n_streams, keep, KV_DIM), F32),
        jax.ShapeDtypeStruct((n_streams, SUBLANES, d), F32),
    ]
    return pl.pallas_call(
        functools.partial(_mixer_sample_body, n_streams=n_streams, n_new=n_new),
        grid=(1,),
        in_specs=[whole(a) for a in args],
        out_specs=[whole(s) for s in out_shape],
        out_shape=out_shape,
        scratch_shapes=[
            pltpu.VMEM((n, d), BF16),
            pltpu.VMEM((n, d), BF16),
            pltpu.VMEM((n, d), F32),
            pltpu.VMEM((n_streams, SUBLANES + n_new, d), F32),
        ],
        compiler_params=pltpu.CompilerParams(
            dimension_semantics=("arbitrary",), vmem_limit_bytes=VMEM_LIMIT_BYTES),
        name="mixer_sample",
    )(*args)


def _sink_columns(sinks_l, nq):
    s = sinks_l.astype(F32).reshape(N_KV_HEADS, PAIRS, 2)
    s = jnp.transpose(s, (0, 2, 1))
    return jnp.repeat(s, nq, axis=2)[..., None]


def kernel(x_prompt, x_sample, cache_k, cache_v, state_conv, norm_ffn1, w1_gate, w1_up, w1_down,
           norm_mix, w_in, b_in, sinks, conv_w, w_o_attn, w_o_conv, w_o, norm_ffn2, w2_gate,
           w2_up, w2_down, norm_final):
    batch, seq, d = x_prompt.shape
    n_streams, n_new, _ = x_sample.shape
    depth = w_in.shape[0]
    keep_s = cache_k.shape[2]
    tile_ffn = 512
    tile_mix = 256

    xp = x_prompt.reshape(batch * seq, d)
    xs = x_sample.reshape(n_streams * n_new, d)
    gf = norm_final.reshape(1, d)
    st_pad = jnp.pad(state_conv, ((0, 0), (0, 0), (SUBLANES - (CONV_WIDTH - 1), 0), (0, 0)))

    kp, vp, cp, ks, vs, cs = [], [], [], [], [], []
    for l in range(depth):
        last = l == depth - 1
        w1 = (norm_ffn1[l].reshape(1, d), w1_gate[l].astype(BF16), w1_up[l].astype(BF16),
              w1_down[l].astype(BF16), gf)
        w2 = (norm_ffn2[l].reshape(1, d), w2_gate[l].astype(BF16), w2_up[l].astype(BF16),
              w2_down[l].astype(BF16), gf)
        wm = (norm_mix[l].reshape(1, d), w_in[l].astype(BF16), b_in[l].reshape(1, -1))
        wo = (conv_w[l], w_o_attn[l].astype(BF16), w_o_conv[l].astype(BF16), w_o[l].astype(BF16))

        xp = _ffn(xp, *w1, tile=tile_ffn, final_norm=False)
        xs = _ffn(xs, *w1, tile=xs.shape[0], final_norm=False)

        xp, k_l, v_l, u_l = _mixer_prompt(
            xp, *wm, _sink_columns(sinks[l], CHUNK), *wo, batch=batch, tile=tile_mix)
        kp.append(k_l.reshape(batch, WINDOW, N_KV_HEADS, HEAD_DIM))
        vp.append(v_l.reshape(batch, WINDOW, N_KV_HEADS, HEAD_DIM))
        cp.append(u_l[:, SUBLANES - (CONV_WIDTH - 1):, :])

        xs, k_l, v_l, u_l = _mixer_sample(
            xs, cache_k[l].reshape(n_streams, keep_s, KV_DIM),
            cache_v[l].reshape(n_streams, keep_s, KV_DIM), st_pad[l],
            *wm, _sink_columns(sinks[l], n_new), *wo)
        ks.append(k_l.reshape(n_streams, keep_s, N_KV_HEADS, HEAD_DIM))
        vs.append(v_l.reshape(n_streams, keep_s, N_KV_HEADS, HEAD_DIM))
        cs.append(u_l[:, SUBLANES - (CONV_WIDTH - 1):, :])

        xp = _ffn(xp, *w2, tile=tile_ffn, final_norm=last)
        xs = _ffn(xs, *w2, tile=xs.shape[0], final_norm=last)

    return (xp.reshape(batch, seq, d), xs.reshape(n_streams, n_new, d),
            jnp.stack(kp), jnp.stack(vp), jnp.stack(cp),
            jnp.stack(ks), jnp.stack(vs), jnp.stack(cs))
```

```python
import functools

import jax
import jax.numpy as jnp
from jax import lax
from jax.experimental import pallas as pl
from jax.experimental.pallas import tpu as pltpu

F32 = jnp.float32
BF16 = jnp.bfloat16

CHUNK = 64
N_HEADS = 16
N_KV_HEADS = 2
HEAD_DIM = 64
GROUP = N_HEADS // N_KV_HEADS
PAIRS = GROUP // 2
KV_DIM = N_KV_HEADS * HEAD_DIM
WINDOW = 128
CONV_WIDTH = 3
EPS = 1e-6
ATTN_SCALE = HEAD_DIM ** -0.5

KEY_SPAN = 256
LANES = 128
SUBLANES = 8
VMEM_LIMIT_BYTES = 56 * 1024 * 1024


def _rms(x, g):
    y = x * lax.rsqrt(jnp.mean(x * x, axis=-1, keepdims=True) + EPS)
    return y * g


def _dot(a, b):
    return jnp.dot(a, b, preferred_element_type=F32)


def _dot_nt(a, b):
    return lax.dot_general(a, b, (((1,), (1,)), ((), ())), preferred_element_type=F32)


def _const_spec(shape):
    nd = len(shape)
    return pl.BlockSpec(shape, lambda *_: (0,) * nd, pipeline_mode=pl.Buffered(1))


def _ffn_body(x_ref, g_ref, wg_ref, wu_ref, wd_ref, gf_ref, o_ref, *, final_norm):
    x = x_ref[...]
    h = _rms(x, g_ref[...]).astype(BF16)
    gate = _dot(h, wg_ref[...])
    up = _dot(h, wu_ref[...])
    act = (gate * jax.nn.sigmoid(gate) * up).astype(BF16)
    out = x + 0.5 * _dot(act, wd_ref[...])
    if final_norm:
        out = _rms(out, gf_ref[...])
    o_ref[...] = out


def _ffn(x2, g, wg, wu, wd, gf, *, tile, final_norm):
    n, d = x2.shape
    dff = wg.shape[1]
    assert n % tile == 0
    return pl.pallas_call(
        functools.partial(_ffn_body, final_norm=final_norm),
        grid=(n // tile,),
        in_specs=[
            pl.BlockSpec((tile, d), lambda i: (i, 0)),
            _const_spec((1, d)),
            _const_spec((d, dff)),
            _const_spec((d, dff)),
            _const_spec((dff, d)),
            _const_spec((1, d)),
        ],
        out_specs=pl.BlockSpec((tile, d), lambda i: (i, 0)),
        out_shape=jax.ShapeDtypeStruct((n, d), F32),
        compiler_params=pltpu.CompilerParams(
            dimension_semantics=("arbitrary",), vmem_limit_bytes=VMEM_LIMIT_BYTES),
        name="ffn_final" if final_norm else "ffn",
    )(x2, g, wg, wu, wd, gf)


def _split_offsets(d):
    sizes = (N_HEADS * HEAD_DIM, KV_DIM, KV_DIM, d, d, d, d, d)
    offs = [0]
    for s in sizes:
        offs.append(offs[-1] + s)
    return offs


def _kv_placements(k2, v2):
    n = k2.shape[0]
    lo = lax.broadcasted_iota(jnp.int32, (n, LANES), 1) < HEAD_DIM
    k2r = pltpu.roll(k2, HEAD_DIM, 1)
    v2r = pltpu.roll(v2, HEAD_DIM, 1)
    ones_lo = jnp.where(lo, 1.0, 0.0)
    ones_hi = jnp.where(lo, 0.0, 1.0)

    def cat(a, b):
        return jnp.concatenate([a, b], axis=1).astype(BF16)

    ka = (jnp.where(lo, k2, 0.0).astype(BF16), jnp.where(lo, k2r, 0.0).astype(BF16))
    kb = (jnp.where(lo, 0.0, k2r).astype(BF16), jnp.where(lo, 0.0, k2).astype(BF16))
    va = (cat(jnp.where(lo, v2, 0.0), ones_lo), cat(jnp.where(lo, v2r, 0.0), ones_lo))
    vb = (cat(jnp.where(lo, 0.0, v2r), ones_hi), cat(jnp.where(lo, 0.0, v2), ones_hi))
    return ka, kb, va, vb


def _attend(qs, at, q_row0, nq, windows, sink_ref, bias):
    ka, kb, va, vb = windows
    qrows = pl.ds(q_row0, nq)
    lo = lax.broadcasted_iota(jnp.int32, (1, LANES), 1) < HEAD_DIM
    for h in range(N_KV_HEADS):
        base = h * GROUP * HEAD_DIM
        q_stack = jnp.concatenate(
            [qs[qrows, base + p * LANES:base + (p + 1) * LANES] for p in range(PAIRS)], axis=0)
        s0 = _dot_nt(q_stack, ka[h]) + bias
        s1 = _dot_nt(q_stack, kb[h]) + bias
        sink0 = sink_ref[h, 0]
        sink1 = sink_ref[h, 1]
        m0 = jnp.maximum(jnp.max(s0, axis=1, keepdims=True), sink0)
        m1 = jnp.maximum(jnp.max(s1, axis=1, keepdims=True), sink1)
        e0 = jnp.exp(s0 - m0).astype(BF16)
        e1 = jnp.exp(s1 - m1).astype(BF16)
        o2 = _dot(e0, va[h]) + _dot(e1, vb[h])
        e_sink = jnp.where(lo, jnp.exp(sink0 - m0), jnp.exp(sink1 - m1))
        o = (o2[:, 0:LANES] / (o2[:, LANES:2 * LANES] + e_sink)).astype(BF16)
        for p in range(PAIRS):
            at[qrows, base + p * LANES:base + (p + 1) * LANES] = o[p * nq:(p + 1) * nq, :]


def _conv_taps(ub, cw_ref, n):
    out = ub[pl.ds(SUBLANES - 2, n), :] * cw_ref[0:1, :]
    out = out + ub[pl.ds(SUBLANES - 1, n), :] * cw_ref[1:2, :]
    return out + ub[pl.ds(SUBLANES, n), :] * cw_ref[2:3, :]


CARRY = KEY_SPAN - CHUNK


def _mixer_prompt_body(x_ref, g_ref, win_ref, bin_ref, sink_ref, cw_ref, woa_ref, woc_ref, wo_ref,
                       xo_ref, kn_ref, vn_ref, un_ref,
                       qs, at, ub, *carry_refs, tile):
    d = x_ref.shape[1]
    offs = _split_offsets(d)
    t = pl.program_id(1)
    cur = lax.rem(t, 2)
    nxt = 1 - cur

    @pl.when(t == 0)
    def _():
        for r in carry_refs:
            r[0] = jnp.zeros(r.shape[1:], BF16)
        ub[0, 0:SUBLANES, :] = jnp.zeros((SUBLANES, d), F32)

    x = x_ref[...]
    h = _rms(x, g_ref[...]).astype(BF16)

    def proj(i):
        return _dot(h, win_ref[:, offs[i]:offs[i + 1]]) + bin_ref[:, offs[i]:offs[i + 1]]

    k2 = proj(1)
    v2 = proj(2)
    kn_ref[0] = k2[tile - WINDOW:, :]
    vn_ref[0] = v2[tile - WINDOW:, :]
    qs[...] = (proj(0) * ATTN_SCALE).astype(BF16)

    placed = [a for pair in _kv_placements(k2, v2) for a in pair]
    carried = [r[cur] for r in carry_refs]

    def window(i, r0):
        if r0 >= CARRY:
            return placed[i][r0 - CARRY:r0 - CARRY + KEY_SPAN, :]
        return jnp.concatenate([carried[i][r0:, :], placed[i][0:r0 + CHUNK, :]], axis=0)

    j = lax.broadcasted_iota(jnp.int32, (1, KEY_SPAN), 1)
    for c in range(tile // CHUNK):
        r0 = c * CHUNK
        pos = t * tile + (r0 - CARRY) + j
        valid = jnp.logical_and(j >= KEY_SPAN - WINDOW - CHUNK, pos >= 0)
        bias = jnp.where(valid, 0.0, -jnp.inf).astype(F32)
        windows = tuple(tuple(window(2 * i + hh, r0) for hh in range(N_KV_HEADS))
                        for i in range(4))
        _attend(qs, at, r0, CHUNK, windows, sink_ref, bias)

    u = proj(4) * proj(5)
    ub[cur, SUBLANES:SUBLANES + tile, :] = u
    conv = (proj(3) * _conv_taps(ub.at[cur], cw_ref, tile)).astype(BF16)
    un_ref[0] = u[tile - SUBLANES:, :]
    ub[nxt, 0:SUBLANES, :] = u[tile - SUBLANES:, :]

    merged = (jax.nn.sigmoid(proj(6)) * _dot(at[...], woa_ref[...])
              + jax.nn.sigmoid(proj(7)) * _dot(conv, woc_ref[...]))
    xo_ref[...] = x + _dot(merged.astype(BF16), wo_ref[...])

    for r, a in zip(carry_refs, placed):
        r[nxt] = a[tile - CARRY:, :]


def _mixer_prompt(x2, g, w_in, b_in, sink_cols, conv_w, woa, woc, wo, *, batch, tile):
    n, d = x2.shape
    seq = n // batch
    assert seq % tile == 0 and tile % CHUNK == 0 and tile >= CARRY and tile >= WINDOW
    nt = seq // tile
    n_in = w_in.shape[1]
    return pl.pallas_call(
        functools.partial(_mixer_prompt_body, tile=tile),
        grid=(batch, nt),
        in_specs=[
            pl.BlockSpec((tile, d), lambda b, t: (b * nt + t, 0)),
            _const_spec((1, d)),
            _const_spec((d, n_in)),
            _const_spec((1, n_in)),
            _const_spec(sink_cols.shape),
            _const_spec(conv_w.shape),
            _const_spec(woa.shape),
            _const_spec(woc.shape),
            _const_spec(wo.shape),
        ],
        out_specs=[
            pl.BlockSpec((tile, d), lambda b, t: (b * nt + t, 0)),
            pl.BlockSpec((1, WINDOW, KV_DIM), lambda b, t: (b, 0, 0)),
            pl.BlockSpec((1, WINDOW, KV_DIM), lambda b, t: (b, 0, 0)),
            pl.BlockSpec((1, SUBLANES, d), lambda b, t: (b, 0, 0)),
        ],
        out_shape=[
            jax.ShapeDtypeStruct((n, d), F32),
            jax.ShapeDtypeStruct((batch, WINDOW, KV_DIM), F32),
            jax.ShapeDtypeStruct((batch, WINDOW, KV_DIM), F32),
            jax.ShapeDtypeStruct((batch, SUBLANES, d), F32),
        ],
        scratch_shapes=[
            pltpu.VMEM((tile, d), BF16),
            pltpu.VMEM((tile, d), BF16),
            pltpu.VMEM((2, SUBLANES + tile, d), F32),
        ] + [pltpu.VMEM((2, CARRY, width), BF16)
             for width in (LANES, LANES, 2 * LANES, 2 * LANES) for _ in range(N_KV_HEADS)],
        compiler_params=pltpu.CompilerParams(
            dimension_semantics=("arbitrary", "arbitrary"), vmem_limit_bytes=VMEM_LIMIT_BYTES),
        name="mixer_prompt",
    )(x2, g, w_in, b_in, sink_cols, conv_w, woa, woc, wo)


def _mixer_sample_body(x_ref, ck_ref, cv_ref, st_ref, g_ref, win_ref, bin_ref, sink_ref, cw_ref,
                       woa_ref, woc_ref, wo_ref,
                       xo_ref, kn_ref, vn_ref, un_ref,
                       qs, at, cs, ub, *, n_streams, n_new):
    d = x_ref.shape[1]
    offs = _split_offsets(d)
    keep = ck_ref.shape[1]
    x = x_ref[...]
    h = _rms(x, g_ref[...]).astype(BF16)

    def proj(i):
        return _dot(h, win_ref[:, offs[i]:offs[i + 1]]) + bin_ref[:, offs[i]:offs[i + 1]]

    kn_all = proj(1)
    vn_all = proj(2)
    qs[...] = (proj(0) * ATTN_SCALE).astype(BF16)
    u_all = proj(4) * proj(5)

    pad0 = keep + n_new
    j = lax.broadcasted_iota(jnp.int32, (1, KEY_SPAN), 1)
    bias = jnp.where(j < pad0, 0.0, -jnp.inf).astype(F32)

    for b in range(n_streams):
        r0 = b * n_new
        k_new = kn_all[r0:r0 + n_new, :]
        v_new = vn_all[r0:r0 + n_new, :]
        kn_ref[b, 0:keep - n_new, :] = ck_ref[b, n_new:keep, :]
        kn_ref[b, keep - n_new:keep, :] = k_new
        vn_ref[b, 0:keep - n_new, :] = cv_ref[b, n_new:keep, :]
        vn_ref[b, keep - n_new:keep, :] = v_new
        old = _kv_placements(ck_ref[b], cv_ref[b])
        new = _kv_placements(k_new, v_new)
        windows = tuple(
            tuple(jnp.concatenate(
                [o[hh], w[hh], jnp.zeros((KEY_SPAN - pad0, o[hh].shape[1]), BF16)], axis=0)
                for hh in range(N_KV_HEADS))
            for o, w in zip(old, new))
        _attend(qs, at, r0, n_new, windows, sink_ref, bias)

        ub[b, 0:SUBLANES, :] = st_ref[b]
        ub[b, SUBLANES:SUBLANES + n_new, :] = u_all[r0:r0 + n_new, :]
        cs[r0:r0 + n_new, :] = _conv_taps(ub.at[b], cw_ref, n_new)
        un_ref[b] = u_all[r0 + n_new - SUBLANES:r0 + n_new, :]

    conv = (proj(3) * cs[...]).astype(BF16)
    merged = (jax.nn.sigmoid(proj(6)) * _dot(at[...], woa_ref[...])
              + jax.nn.sigmoid(proj(7)) * _dot(conv, woc_ref[...]))
    xo_ref[...] = x + _dot(merged.astype(BF16), wo_ref[...])


def _mixer_sample(x2, ck, cv, st, g, w_in, b_in, sink_cols, conv_w, woa, woc, wo):
    n, d = x2.shape
    n_streams, keep, _ = ck.shape
    n_new = n // n_streams
    assert keep + n_new <= KEY_SPAN and n_new % SUBLANES == 0 and keep >= n_new
    args = (x2, ck, cv, st, g, w_in, b_in, sink_cols, conv_w, woa, woc, wo)

    def whole(a):
        nd = a.ndim
        return pl.BlockSpec(a.shape, lambda i: (0,) * nd)

    out_shape = [
        jax.ShapeDtypeStruct((n, d), F32),
        jax.ShapeDtypeStruct((n_streams, keep, KV_DIM), F32),
        jax.ShapeDtypeStruct((n_streams, keep, KV_DIM), F32),
        jax.ShapeDtypeStruct((n_streams, SUBLANES, d), F32),
    ]
    return pl.pallas_call(
        functools.partial(_mixer_sample_body, n_streams=n_streams, n_new=n_new),
        grid=(1,),
        in_specs=[whole(a) for a in args],
        out_specs=[whole(s) for s in out_shape],
        out_shape=out_shape,
        scratch_shapes=[
            pltpu.VMEM((n, d), BF16),
            pltpu.VMEM((n, d), BF16),
            pltpu.VMEM((n, d), F32),
            pltpu.VMEM((n_streams, SUBLANES + n_new, d), F32),
        ],
        compiler_params=pltpu.CompilerParams(
            dimension_semantics=("arbitrary",), vmem_limit_bytes=VMEM_LIMIT_BYTES),
        name="mixer_sample",
    )(*args)


def _sink_columns(sinks_l, nq):
    s = sinks_l.astype(F32).reshape(N_KV_HEADS, PAIRS, 2)
    s = jnp.transpose(s, (0, 2, 1))
    return jnp.repeat(s, nq, axis=2)[..., None]


def kernel(x_prompt, x_sample, cache_k, cache_v, state_conv, norm_ffn1, w1_gate, w1_up, w1_down,
           norm_mix, w_in, b_in, sinks, conv_w, w_o_attn, w_o_conv, w_o, norm_ffn2, w2_gate,
           w2_up, w2_down, norm_final):
    batch, seq, d = x_prompt.shape
    n_streams, n_new, _ = x_sample.shape
    depth = w_in.shape[0]
    keep_s = cache_k.shape[2]
    tile_ffn = 512
    tile_mix = 256

    xp = x_prompt.reshape(batch * seq, d)
    xs = x_sample.reshape(n_streams * n_new, d)
    gf = norm_final.reshape(1, d)
    st_pad = jnp.pad(state_conv, ((0, 0), (0, 0), (SUBLANES - (CONV_WIDTH - 1), 0), (0, 0)))

    kp, vp, cp, ks, vs, cs = [], [], [], [], [], []
    for l in range(depth):
        last = l == depth - 1
        w1 = (norm_ffn1[l].reshape(1, d), w1_gate[l].astype(BF16), w1_up[l].astype(BF16),
              w1_down[l].astype(BF16), gf)
        w2 = (norm_ffn2[l].reshape(1, d), w2_gate[l].astype(BF16), w2_up[l].astype(BF16),
              w2_down[l].astype(BF16), gf)
        wm = (norm_mix[l].reshape(1, d), w_in[l].astype(BF16), b_in[l].reshape(1, -1))
        wo = (conv_w[l], w_o_attn[l].astype(BF16), w_o_conv[l].astype(BF16), w_o[l].astype(BF16))

        xp = _ffn(xp, *w1, tile=tile_ffn, final_norm=False)
        xs = _ffn(xs, *w1, tile=xs.shape[0], final_norm=False)

        xp, k_l, v_l, u_l = _mixer_prompt(
            xp, *wm, _sink_columns(sinks[l], CHUNK), *wo, batch=batch, tile=tile_mix)
        kp.append(k_l.reshape(batch, WINDOW, N_KV_HEADS, HEAD_DIM))
        vp.append(v_l.reshape(batch, WINDOW, N_KV_HEADS, HEAD_DIM))
        cp.append(u_l[:, SUBLANES - (CONV_WIDTH - 1):, :])

        xs, k_l, v_l, u_l = _mixer_sample(
            xs, cache_k[l].reshape(n_streams, keep_s, KV_DIM),
            cache_v[l].reshape(n_streams, keep_s, KV_DIM), st_pad[l],
            *wm, _sink_columns(sinks[l], n_new), *wo)
        ks.append(k_l.reshape(n_streams, keep_s, N_KV_HEADS, HEAD_DIM))
        vs.append(v_l.reshape(n_streams, keep_s, N_KV_HEADS, HEAD_DIM))
        cs.append(u_l[:, SUBLANES - (CONV_WIDTH - 1):, :])

        xp = _ffn(xp, *w2, tile=tile_ffn, final_norm=last)
        xs = _ffn(xs, *w2, tile=xs.shape[0], final_norm=last)

    return (xp.reshape(batch, seq, d), xs.reshape(n_streams, n_new, d),
            jnp.stack(kp), jnp.stack(vp), jnp.stack(cp),
            jnp.stack(ks), jnp.stack(vs), jnp.stack(cs))
```

```python
import functools

import jax
import jax.numpy as jnp
from jax import lax
from jax.experimental import pallas as pl
from jax.experimental.pallas import tpu as pltpu

F32 = jnp.float32
BF16 = jnp.bfloat16

CHUNK = 64
N_HEADS = 16
N_KV_HEADS = 2
HEAD_DIM = 64
GROUP = N_HEADS // N_KV_HEADS
PAIRS = GROUP // 2
KV_DIM = N_KV_HEADS * HEAD_DIM
WINDOW = 128
CONV_WIDTH = 3
EPS = 1e-6
ATTN_SCALE = HEAD_DIM ** -0.5

KEY_SPAN = 256
LANES = 128
SUBLANES = 8
VMEM_LIMIT_BYTES = 56 * 1024 * 1024


def _rms(x, g):
    y = x * lax.rsqrt(jnp.mean(x * x, axis=-1, keepdims=True) + EPS)
    return y * g


def _dot(a, b):
    return jnp.dot(a, b, preferred_element_type=F32)


def _dot_nt(a, b):
    return lax.dot_general(a, b, (((1,), (1,)), ((), ())), preferred_element_type=F32)


def _const_spec(shape):
    nd = len(shape)
    return pl.BlockSpec(shape, lambda *_: (0,) * nd, pipeline_mode=pl.Buffered(1))


def _ffn_body(x_ref, g_ref, wg_ref, wu_ref, wd_ref, gf_ref, o_ref, *, final_norm):
    x = x_ref[...]
    h = _rms(x, g_ref[...]).astype(BF16)
    gate = _dot(h, wg_ref[...])
    up = _dot(h, wu_ref[...])
    act = (gate * jax.nn.sigmoid(gate) * up).astype(BF16)
    out = x + 0.5 * _dot(act, wd_ref[...])
    if final_norm:
        out = _rms(out, gf_ref[...])
    o_ref[...] = out


def _ffn(x2, g, wg, wu, wd, gf, *, tile, final_norm):
    n, d = x2.shape
    dff = wg.shape[1]
    assert n % tile == 0
    return pl.pallas_call(
        functools.partial(_ffn_body, final_norm=final_norm),
        grid=(n // tile,),
        in_specs=[
            pl.BlockSpec((tile, d), lambda i: (i, 0)),
            _const_spec((1, d)),
            _const_spec((d, dff)),
            _const_spec((d, dff)),
            _const_spec((dff, d)),
            _const_spec((1, d)),
        ],
        out_specs=pl.BlockSpec((tile, d), lambda i: (i, 0)),
        out_shape=jax.ShapeDtypeStruct((n, d), F32),
        compiler_params=pltpu.CompilerParams(
            dimension_semantics=("arbitrary",), vmem_limit_bytes=VMEM_LIMIT_BYTES),
        name="ffn_final" if final_norm else "ffn",
    )(x2, g, wg, wu, wd, gf)


def _split_offsets(d):
    sizes = (N_HEADS * HEAD_DIM, KV_DIM, KV_DIM, d, d, d, d, d)
    offs = [0]
    for s in sizes:
        offs.append(offs[-1] + s)
    return offs


def _kv_placements(k2, v2):
    n = k2.shape[0]
    lo = lax.broadcasted_iota(jnp.int32, (n, LANES), 1) < HEAD_DIM
    k2r = pltpu.roll(k2, HEAD_DIM, 1)
    v2r = pltpu.roll(v2, HEAD_DIM, 1)
    ones_lo = jnp.where(lo, 1.0, 0.0)
    ones_hi = jnp.where(lo, 0.0, 1.0)

    def cat(a, b):
        return jnp.concatenate([a, b], axis=1).astype(BF16)

    ka = (jnp.where(lo, k2, 0.0).astype(BF16), jnp.where(lo, k2r, 0.0).astype(BF16))
    kb = (jnp.where(lo, 0.0, k2r).astype(BF16), jnp.where(lo, 0.0, k2).astype(BF16))
    va = (cat(jnp.where(lo, v2, 0.0), ones_lo), cat(jnp.where(lo, v2r, 0.0), ones_lo))
    vb = (cat(jnp.where(lo, 0.0, v2r), ones_hi), cat(jnp.where(lo, 0.0, v2), ones_hi))
    return ka, kb, va, vb


def _attn_probs(qs, q_row0, nq, ka, kb, sink_ref, bias):
    qrows = pl.ds(q_row0, nq)
    lo = lax.broadcasted_iota(jnp.int32, (1, LANES), 1) < HEAD_DIM
    out = []
    for h in range(N_KV_HEADS):
        base = h * GROUP * HEAD_DIM
        q_stack = jnp.concatenate(
            [qs[qrows, base + p * LANES:base + (p + 1) * LANES] for p in range(PAIRS)], axis=0)
        s0 = _dot_nt(q_stack, ka[h]) + bias
        s1 = _dot_nt(q_stack, kb[h]) + bias
        sink0 = sink_ref[h, 0]
        sink1 = sink_ref[h, 1]
        m0 = jnp.maximum(jnp.max(s0, axis=1, keepdims=True), sink0)
        m1 = jnp.maximum(jnp.max(s1, axis=1, keepdims=True), sink1)
        e0 = jnp.exp(s0 - m0).astype(BF16)
        e1 = jnp.exp(s1 - m1).astype(BF16)
        e_sink = jnp.where(lo, jnp.exp(sink0 - m0), jnp.exp(sink1 - m1))
        out.append((e0, e1, e_sink))
    return out


def _attn_output(at, q_row0, nq, probs, va, vb):
    qrows = pl.ds(q_row0, nq)
    for h in range(N_KV_HEADS):
        base = h * GROUP * HEAD_DIM
        e0, e1, e_sink = probs[h]
        o2 = _dot(e0, va[h]) + _dot(e1, vb[h])
        o = (o2[:, 0:LANES] / (o2[:, LANES:2 * LANES] + e_sink)).astype(BF16)
        for p in range(PAIRS):
            at[qrows, base + p * LANES:base + (p + 1) * LANES] = o[p * nq:(p + 1) * nq, :]


def _conv_taps(ub, cw_ref, n):
    out = ub[pl.ds(SUBLANES - 2, n), :] * cw_ref[0:1, :]
    out = out + ub[pl.ds(SUBLANES - 1, n), :] * cw_ref[1:2, :]
    return out + ub[pl.ds(SUBLANES, n), :] * cw_ref[2:3, :]


CARRY = KEY_SPAN - CHUNK


def _mixer_prompt_body(x_ref, g_ref, win_ref, bin_ref, sink_ref, cw_ref, woa_ref, woc_ref, wo_ref,
                       xo_ref, kn_ref, vn_ref, un_ref,
                       qs, at, ub, *carry_refs, tile):
    d = x_ref.shape[1]
    offs = _split_offsets(d)
    t = pl.program_id(1)
    cur = lax.rem(t, 2)
    nxt = 1 - cur

    @pl.when(t == 0)
    def _():
        for r in carry_refs:
            r[0] = jnp.zeros(r.shape[1:], BF16)
        ub[0, 0:SUBLANES, :] = jnp.zeros((SUBLANES, d), F32)

    x = x_ref[...]
    h = _rms(x, g_ref[...]).astype(BF16)

    def proj(i):
        return _dot(h, win_ref[:, offs[i]:offs[i + 1]]) + bin_ref[:, offs[i]:offs[i + 1]]

    k2 = proj(1)
    v2 = proj(2)
    kn_ref[0] = k2[tile - WINDOW:, :]
    vn_ref[0] = v2[tile - WINDOW:, :]
    qs[...] = (proj(0) * ATTN_SCALE).astype(BF16)

    placed = [a for pair in _kv_placements(k2, v2) for a in pair]
    carried = [r[cur] for r in carry_refs]

    def window(i, r0):
        if r0 >= CARRY:
            return placed[i][r0 - CARRY:r0 - CARRY + KEY_SPAN, :]
        return jnp.concatenate([carried[i][r0:, :], placed[i][0:r0 + CHUNK, :]], axis=0)

    def windows(i, r0):
        return tuple(window(2 * i + hh, r0) for hh in range(N_KV_HEADS))

    j = lax.broadcasted_iota(jnp.int32, (1, KEY_SPAN), 1)
    probs = []
    for c in range(tile // CHUNK):
        r0 = c * CHUNK
        pos = t * tile + (r0 - CARRY) + j
        valid = jnp.logical_and(j >= KEY_SPAN - WINDOW - CHUNK, pos >= 0)
        bias = jnp.where(valid, 0.0, -jnp.inf).astype(F32)
        probs.append(_attn_probs(qs, r0, CHUNK, windows(0, r0), windows(1, r0), sink_ref, bias))

    u = proj(4) * proj(5)
    ub[cur, SUBLANES:SUBLANES + tile, :] = u
    conv = (proj(3) * _conv_taps(ub.at[cur], cw_ref, tile)).astype(BF16)
    un_ref[0] = u[tile - SUBLANES:, :]
    ub[nxt, 0:SUBLANES, :] = u[tile - SUBLANES:, :]
    gate_attn = jax.nn.sigmoid(proj(6))
    gate_conv = jax.nn.sigmoid(proj(7))
    y_conv = gate_conv * _dot(conv, woc_ref[...])

    for c in range(tile // CHUNK):
        r0 = c * CHUNK
        _attn_output(at, r0, CHUNK, probs[c], windows(2, r0), windows(3, r0))

    merged = gate_attn * _dot(at[...], woa_ref[...]) + y_conv
    xo_ref[...] = x + _dot(merged.astype(BF16), wo_ref[...])

    for r, a in zip(carry_refs, placed):
        r[nxt] = a[tile - CARRY:, :]


def _mixer_prompt(x2, g, w_in, b_in, sink_cols, conv_w, woa, woc, wo, *, batch, tile):
    n, d = x2.shape
    seq = n // batch
    assert seq % tile == 0 and tile % CHUNK == 0 and tile >= CARRY and tile >= WINDOW
    nt = seq // tile
    n_in = w_in.shape[1]
    return pl.pallas_call(
        functools.partial(_mixer_prompt_body, tile=tile),
        grid=(batch, nt),
        in_specs=[
            pl.BlockSpec((tile, d), lambda b, t: (b * nt + t, 0)),
            _const_spec((1, d)),
            _const_spec((d, n_in)),
            _const_spec((1, n_in)),
            _const_spec(sink_cols.shape),
            _const_spec(conv_w.shape),
            _const_spec(woa.shape),
            _const_spec(woc.shape),
            _const_spec(wo.shape),
        ],
        out_specs=[
            pl.BlockSpec((tile, d), lambda b, t: (b * nt + t, 0)),
            pl.BlockSpec((1, WINDOW, KV_DIM), lambda b, t: (b, 0, 0)),
            pl.BlockSpec((1, WINDOW, KV_DIM), lambda b, t: (b, 0, 0)),
            pl.BlockSpec((1, SUBLANES, d), lambda b, t: (b, 0, 0)),
        ],
        out_shape=[
            jax.ShapeDtypeStruct((n, d), F32),
            jax.ShapeDtypeStruct((batch, WINDOW, KV_DIM), F32),
            jax.ShapeDtypeStruct((batch, WINDOW, KV_DIM), F32),
            jax.ShapeDtypeStruct((batch, SUBLANES, d), F32),
        ],
        scratch_shapes=[
            pltpu.VMEM((tile, d), BF16),
            pltpu.VMEM((tile, d), BF16),
            pltpu.VMEM((2, SUBLANES + tile, d), F32),
        ] + [pltpu.VMEM((2, CARRY, width), BF16)
             for width in (LANES, LANES, 2 * LANES, 2 * LANES) for _ in range(N_KV_HEADS)],
        compiler_params=pltpu.CompilerParams(
            dimension_semantics=("arbitrary", "arbitrary"), vmem_limit_bytes=VMEM_LIMIT_BYTES),
        name="mixer_prompt",
    )(x2, g, w_in, b_in, sink_cols, conv_w, woa, woc, wo)


def _mixer_sample_body(x_ref, ck_ref, cv_ref, st_ref, g_ref, win_ref, bin_ref, sink_ref, cw_ref,
                       woa_ref, woc_ref, wo_ref,
                       xo_ref, kn_ref, vn_ref, un_ref,
                       qs, at, cs, ub, *, n_streams, n_new):
    d = x_ref.shape[1]
    offs = _split_offsets(d)
    keep = ck_ref.shape[1]
    x = x_ref[...]
    h = _rms(x, g_ref[...]).astype(BF16)

    def proj(i):
        return _dot(h, win_ref[:, offs[i]:offs[i + 1]]) + bin_ref[:, offs[i]:offs[i + 1]]

    kn_all = proj(1)
    vn_all = proj(2)
    qs[...] = (proj(0) * ATTN_SCALE).astype(BF16)
    u_all = proj(4) * proj(5)

    pad0 = keep + n_new
    j = lax.broadcasted_iota(jnp.int32, (1, KEY_SPAN), 1)
    bias = jnp.where(j < pad0, 0.0, -jnp.inf).astype(F32)

    for b in range(n_streams):
        r0 = b * n_new
        k_new = kn_all[r0:r0 + n_new, :]
        v_new = vn_all[r0:r0 + n_new, :]
        kn_ref[b, 0:keep - n_new, :] = ck_ref[b, n_new:keep, :]
        kn_ref[b, keep - n_new:keep, :] = k_new
        vn_ref[b, 0:keep - n_new, :] = cv_ref[b, n_new:keep, :]
        vn_ref[b, keep - n_new:keep, :] = v_new
        old = _kv_placements(ck_ref[b], cv_ref[b])
        new = _kv_placements(k_new, v_new)
        ka, kb, va, vb = tuple(
            tuple(jnp.concatenate(
                [o[hh], w[hh], jnp.zeros((KEY_SPAN - pad0, o[hh].shape[1]), BF16)], axis=0)
                for hh in range(N_KV_HEADS))
            for o, w in zip(old, new))
        probs = _attn_probs(qs, r0, n_new, ka, kb, sink_ref, bias)
        _attn_output(at, r0, n_new, probs, va, vb)

        ub[b, 0:SUBLANES, :] = st_ref[b]
        ub[b, SUBLANES:SUBLANES + n_new, :] = u_all[r0:r0 + n_new, :]
        cs[r0:r0 + n_new, :] = _conv_taps(ub.at[b], cw_ref, n_new)
        un_ref[b] = u_all[r0 + n_new - SUBLANES:r0 + n_new, :]

    conv = (proj(3) * cs[...]).astype(BF16)
    merged = (jax.nn.sigmoid(proj(6)) * _dot(at[...], woa_ref[...])
              + jax.nn.sigmoid(proj(7)) * _dot(conv, woc_ref[...]))
    xo_ref[...] = x + _dot(merged.astype(BF16), wo_ref[...])


def _mixer_sample(x2, ck, cv, st, g, w_in, b_in, sink_cols, conv_w, woa, woc, wo):
    n, d = x2.shape
    n_streams, keep, _ = ck.shape
    n_new = n // n_streams
    assert keep + n_new <= KEY_SPAN and n_new % SUBLANES == 0 and keep >= n_new
    args = (x2, ck, cv, st, g, w_in, b_in, sink_cols, conv_w, woa, woc, wo)

    def whole(a):
        nd = a.ndim
        return pl.BlockSpec(a.shape, lambda i: (0,) * nd)

    out_shape = [
        jax.ShapeDtypeStruct((n, d), F32),
        jax.ShapeDtypeStruct((n_streams, keep, KV_DIM), F32),
        jax.ShapeDtypeStruct((n_streams, keep, KV_DIM), F32),
        jax.ShapeDtypeStruct((n_streams, SUBLANES, d), F32),
    ]
    return pl.pallas_call(
        functools.partial(_mixer_sample_body, n_streams=n_streams, n_new=n_new),
        grid=(1,),
        in_specs=[whole(a) for a in args],
        out_specs=[whole(s) for s in out_shape],
        out_shape=out_shape,
        scratch_shapes=[
            pltpu.VMEM((n, d), BF16),
            pltpu.VMEM((n, d), BF16),
            pltpu.VMEM((n, d), F32),
            pltpu.VMEM((n_streams, SUBLANES + n_new, d), F32),
        ],
        compiler_params=pltpu.CompilerParams(
            dimension_semantics=("arbitrary",), vmem_limit_bytes=VMEM_LIMIT_BYTES),
        name="mixer_sample",
    )(*args)


def _sink_columns(sinks_l, nq):
    s = sinks_l.astype(F32).reshape(N_KV_HEADS, PAIRS, 2)
    s = jnp.transpose(s, (0, 2, 1))
    return jnp.repeat(s, nq, axis=2)[..., None]


def kernel(x_prompt, x_sample, cache_k, cache_v, state_conv, norm_ffn1, w1_gate, w1_up, w1_down,
           norm_mix, w_in, b_in, sinks, conv_w, w_o_attn, w_o_conv, w_o, norm_ffn2, w2_gate,
           w2_up, w2_down, norm_final):
    batch, seq, d = x_prompt.shape
    n_streams, n_new, _ = x_sample.shape
    depth = w_in.shape[0]
    keep_s = cache_k.shape[2]
    tile_ffn = 512
    tile_mix = 512

    xp = x_prompt.reshape(batch * seq, d)
    xs = x_sample.reshape(n_streams * n_new, d)
    gf = norm_final.reshape(1, d)
    st_pad = jnp.pad(state_conv, ((0, 0), (0, 0), (SUBLANES - (CONV_WIDTH - 1), 0), (0, 0)))

    kp, vp, cp, ks, vs, cs = [], [], [], [], [], []
    for l in range(depth):
        last = l == depth - 1
        w1 = (norm_ffn1[l].reshape(1, d), w1_gate[l].astype(BF16), w1_up[l].astype(BF16),
              w1_down[l].astype(BF16), gf)
        w2 = (norm_ffn2[l].reshape(1, d), w2_gate[l].astype(BF16), w2_up[l].astype(BF16),
              w2_down[l].astype(BF16), gf)
        wm = (norm_mix[l].reshape(1, d), w_in[l].astype(BF16), b_in[l].reshape(1, -1))
        wo = (conv_w[l], w_o_attn[l].astype(BF16), w_o_conv[l].astype(BF16), w_o[l].astype(BF16))

        xp = _ffn(xp, *w1, tile=tile_ffn, final_norm=False)
        xs = _ffn(xs, *w1, tile=xs.shape[0], final_norm=False)

        xp, k_l, v_l, u_l = _mixer_prompt(
            xp, *wm, _sink_columns(sinks[l], CHUNK), *wo, batch=batch, tile=tile_mix)
        kp.append(k_l.reshape(batch, WINDOW, N_KV_HEADS, HEAD_DIM))
        vp.append(v_l.reshape(batch, WINDOW, N_KV_HEADS, HEAD_DIM))
        cp.append(u_l[:, SUBLANES - (CONV_WIDTH - 1):, :])

        xs, k_l, v_l, u_l = _mixer_sample(
            xs, cache_k[l].reshape(n_streams, keep_s, KV_DIM),
            cache_v[l].reshape(n_streams, keep_s, KV_DIM), st_pad[l],
            *wm, _sink_columns(sinks[l], n_new), *wo)
        ks.append(k_l.reshape(n_streams, keep_s, N_KV_HEADS, HEAD_DIM))
        vs.append(v_l.reshape(n_streams, keep_s, N_KV_HEADS, HEAD_DIM))
        cs.append(u_l[:, SUBLANES - (CONV_WIDTH - 1):, :])

        xp = _ffn(xp, *w2, tile=tile_ffn, final_norm=last)
        xs = _ffn(xs, *w2, tile=xs.shape[0], final_norm=last)

    return (xp.reshape(batch, seq, d), xs.reshape(n_streams, n_new, d),
            jnp.stack(kp), jnp.stack(vp), jnp.stack(cp),
            jnp.stack(ks), jnp.stack(vs), jnp.stack(cs))
```

```python
import functools

import jax
import jax.numpy as jnp
from jax import lax
from jax.experimental import pallas as pl
from jax.experimental.pallas import tpu as pltpu

F32 = jnp.float32
BF16 = jnp.bfloat16

CHUNK = 64
N_HEADS = 16
N_KV_HEADS = 2
HEAD_DIM = 64
GROUP = N_HEADS // N_KV_HEADS
PAIRS = GROUP // 2
KV_DIM = N_KV_HEADS * HEAD_DIM
WINDOW = 128
CONV_WIDTH = 3
EPS = 1e-6
ATTN_SCALE = HEAD_DIM ** -0.5

KEY_SPAN = 256
LANES = 128
SUBLANES = 8
VMEM_LIMIT_BYTES = 56 * 1024 * 1024


def _rms(x, g):
    y = x * lax.rsqrt(jnp.mean(x * x, axis=-1, keepdims=True) + EPS)
    return y * g


def _dot(a, b):
    return jnp.dot(a, b, preferred_element_type=F32)


def _dot_nt(a, b):
    return lax.dot_general(a, b, (((1,), (1,)), ((), ())), preferred_element_type=F32)


def _const_spec(shape):
    nd = len(shape)
    return pl.BlockSpec(shape, lambda *_: (0,) * nd, pipeline_mode=pl.Buffered(1))


def _layer_spec(stacked, layer):
    rest = stacked.shape[1:]
    return pl.BlockSpec((None,) + rest, lambda *_: (layer,) + (0,) * len(rest),
                        pipeline_mode=pl.Buffered(1))


def _ffn_body(x_ref, g_ref, wg_ref, wu_ref, wd_ref, gf_ref, o_ref, *, final_norm, sub):
    blocks = [pl.ds(r, sub) for r in range(0, x_ref.shape[0], sub)]
    gated = []
    for rows in blocks:
        h = _rms(x_ref[rows, :], g_ref[...]).astype(BF16)
        gated.append((_dot(h, wg_ref[...]), _dot(h, wu_ref[...])))
    for rows, (gate, up) in zip(blocks, gated):
        act = (gate * jax.nn.sigmoid(gate) * up).astype(BF16)
        out = x_ref[rows, :] + 0.5 * _dot(act, wd_ref[...])
        if final_norm:
            out = _rms(out, gf_ref[...])
        o_ref[rows, :] = out


def _ffn(x2, g, wg, wu, wd, gf, *, layer, tile, sub, final_norm):
    n, d = x2.shape
    assert n % tile == 0 and tile % sub == 0
    return pl.pallas_call(
        functools.partial(_ffn_body, final_norm=final_norm, sub=sub),
        grid=(n // tile,),
        in_specs=[
            pl.BlockSpec((tile, d), lambda i: (i, 0)),
            _layer_spec(g, layer),
            _layer_spec(wg, layer),
            _layer_spec(wu, layer),
            _layer_spec(wd, layer),
            _const_spec((1, d)),
        ],
        out_specs=pl.BlockSpec((tile, d), lambda i: (i, 0)),
        out_shape=jax.ShapeDtypeStruct((n, d), F32),
        compiler_params=pltpu.CompilerParams(
            dimension_semantics=("arbitrary",), vmem_limit_bytes=VMEM_LIMIT_BYTES),
        name="ffn_final" if final_norm else "ffn",
    )(x2, g, wg, wu, wd, gf)


def _split_offsets(d):
    sizes = (N_HEADS * HEAD_DIM, KV_DIM, KV_DIM, d, d, d, d, d)
    offs = [0]
    for s in sizes:
        offs.append(offs[-1] + s)
    return offs


def _kv_placements(k2, v2):
    n = k2.shape[0]
    lo = lax.broadcasted_iota(jnp.int32, (n, LANES), 1) < HEAD_DIM
    k2r = pltpu.roll(k2, HEAD_DIM, 1)
    v2r = pltpu.roll(v2, HEAD_DIM, 1)
    ones_lo = jnp.where(lo, 1.0, 0.0)
    ones_hi = jnp.where(lo, 0.0, 1.0)

    def cat(a, b):
        return jnp.concatenate([a, b], axis=1).astype(BF16)

    ka = (jnp.where(lo, k2, 0.0).astype(BF16), jnp.where(lo, k2r, 0.0).astype(BF16))
    kb = (jnp.where(lo, 0.0, k2r).astype(BF16), jnp.where(lo, 0.0, k2).astype(BF16))
    va = (cat(jnp.where(lo, v2, 0.0), ones_lo), cat(jnp.where(lo, v2r, 0.0), ones_lo))
    vb = (cat(jnp.where(lo, 0.0, v2r), ones_hi), cat(jnp.where(lo, 0.0, v2), ones_hi))
    return ka, kb, va, vb


def _attn_probs(qs, q_row0, nq, ka, kb, sink_ref, bias):
    qrows = pl.ds(q_row0, nq)
    lo = lax.broadcasted_iota(jnp.int32, (1, LANES), 1) < HEAD_DIM
    out = []
    for h in range(N_KV_HEADS):
        base = h * GROUP * HEAD_DIM
        q_stack = jnp.concatenate(
            [qs[qrows, base + p * LANES:base + (p + 1) * LANES] for p in range(PAIRS)], axis=0)
        s0 = _dot_nt(q_stack, ka[h]) + bias
        s1 = _dot_nt(q_stack, kb[h]) + bias
        sink0 = sink_ref[h, 0]
        sink1 = sink_ref[h, 1]
        m0 = jnp.maximum(jnp.max(s0, axis=1, keepdims=True), sink0)
        m1 = jnp.maximum(jnp.max(s1, axis=1, keepdims=True), sink1)
        e0 = jnp.exp(s0 - m0).astype(BF16)
        e1 = jnp.exp(s1 - m1).astype(BF16)
        e_sink = jnp.where(lo, jnp.exp(sink0 - m0), jnp.exp(sink1 - m1))
        out.append((e0, e1, e_sink))
    return out


def _attn_output(at, q_row0, nq, probs, va, vb):
    qrows = pl.ds(q_row0, nq)
    for h in range(N_KV_HEADS):
        base = h * GROUP * HEAD_DIM
        e0, e1, e_sink = probs[h]
        o2 = _dot(e0, va[h]) + _dot(e1, vb[h])
        o = (o2[:, 0:LANES] / (o2[:, LANES:2 * LANES] + e_sink)).astype(BF16)
        for p in range(PAIRS):
            at[qrows, base + p * LANES:base + (p + 1) * LANES] = o[p * nq:(p + 1) * nq, :]


def _conv_taps(ub, cw_ref, n):
    out = ub[pl.ds(SUBLANES - 2, n), :] * cw_ref[0:1, :]
    out = out + ub[pl.ds(SUBLANES - 1, n), :] * cw_ref[1:2, :]
    return out + ub[pl.ds(SUBLANES, n), :] * cw_ref[2:3, :]


CARRY = KEY_SPAN - CHUNK


def _mixer_prompt_body(x_ref, g_ref, win_ref, bin_ref, sink_ref, cw_ref, woa_ref, woc_ref, wo_ref,
                       xo_ref, kn_ref, vn_ref, un_ref,
                       qs, at, ub, *carry_refs, tile):
    d = x_ref.shape[1]
    offs = _split_offsets(d)
    t = pl.program_id(1)
    cur = lax.rem(t, 2)
    nxt = 1 - cur

    @pl.when(t == 0)
    def _():
        for r in carry_refs:
            r[0] = jnp.zeros(r.shape[1:], BF16)
        ub[0, 0:SUBLANES, :] = jnp.zeros((SUBLANES, d), F32)

    x = x_ref[...]
    h = _rms(x, g_ref[...]).astype(BF16)

    def proj(i):
        return _dot(h, win_ref[:, offs[i]:offs[i + 1]]) + bin_ref[:, offs[i]:offs[i + 1]]

    k2 = proj(1)
    v2 = proj(2)
    kn_ref[0] = k2[tile - WINDOW:, :]
    vn_ref[0] = v2[tile - WINDOW:, :]
    qs[...] = (proj(0) * ATTN_SCALE).astype(BF16)

    placed = [a for pair in _kv_placements(k2, v2) for a in pair]
    carried = [r[cur] for r in carry_refs]

    def window(i, r0):
        if r0 >= CARRY:
            return placed[i][r0 - CARRY:r0 - CARRY + KEY_SPAN, :]
        return jnp.concatenate([carried[i][r0:, :], placed[i][0:r0 + CHUNK, :]], axis=0)

    def windows(i, r0):
        return tuple(window(2 * i + hh, r0) for hh in range(N_KV_HEADS))

    j = lax.broadcasted_iota(jnp.int32, (1, KEY_SPAN), 1)
    probs = []
    for c in range(tile // CHUNK):
        r0 = c * CHUNK
        pos = t * tile + (r0 - CARRY) + j
        valid = jnp.logical_and(j >= KEY_SPAN - WINDOW - CHUNK, pos >= 0)
        bias = jnp.where(valid, 0.0, -jnp.inf).astype(F32)
        probs.append(_attn_probs(qs, r0, CHUNK, windows(0, r0), windows(1, r0), sink_ref, bias))

    u = proj(4) * proj(5)
    ub[cur, SUBLANES:SUBLANES + tile, :] = u
    conv = (proj(3) * _conv_taps(ub.at[cur], cw_ref, tile)).astype(BF16)
    un_ref[0] = u[tile - SUBLANES:, :]
    ub[nxt, 0:SUBLANES, :] = u[tile - SUBLANES:, :]
    gate_attn = jax.nn.sigmoid(proj(6))
    gate_conv = jax.nn.sigmoid(proj(7))
    y_conv = gate_conv * _dot(conv, woc_ref[...])

    for c in range(tile // CHUNK):
        r0 = c * CHUNK
        _attn_output(at, r0, CHUNK, probs[c], windows(2, r0), windows(3, r0))

    merged = gate_attn * _dot(at[...], woa_ref[...]) + y_conv
    xo_ref[...] = x + _dot(merged.astype(BF16), wo_ref[...])

    for r, a in zip(carry_refs, placed):
        r[nxt] = a[tile - CARRY:, :]


def _mixer_prompt(x2, g, w_in, b_in, sink_cols, conv_w, woa, woc, wo, *, layer, batch, tile):
    n, d = x2.shape
    seq = n // batch
    assert seq % tile == 0 and tile % CHUNK == 0 and tile >= CARRY and tile >= WINDOW
    nt = seq // tile
    return pl.pallas_call(
        functools.partial(_mixer_prompt_body, tile=tile),
        grid=(batch, nt),
        in_specs=[
            pl.BlockSpec((tile, d), lambda b, t: (b * nt + t, 0)),
            _layer_spec(g, layer),
            _layer_spec(w_in, layer),
            _layer_spec(b_in, layer),
            _const_spec(sink_cols.shape),
            _layer_spec(conv_w, layer),
            _layer_spec(woa, layer),
            _layer_spec(woc, layer),
            _layer_spec(wo, layer),
        ],
        out_specs=[
            pl.BlockSpec((tile, d), lambda b, t: (b * nt + t, 0)),
            pl.BlockSpec((1, WINDOW, KV_DIM), lambda b, t: (b, 0, 0)),
            pl.BlockSpec((1, WINDOW, KV_DIM), lambda b, t: (b, 0, 0)),
            pl.BlockSpec((1, SUBLANES, d), lambda b, t: (b, 0, 0)),
        ],
        out_shape=[
            jax.ShapeDtypeStruct((n, d), F32),
            jax.ShapeDtypeStruct((batch, WINDOW, KV_DIM), F32),
            jax.ShapeDtypeStruct((batch, WINDOW, KV_DIM), F32),
            jax.ShapeDtypeStruct((batch, SUBLANES, d), F32),
        ],
        scratch_shapes=[
            pltpu.VMEM((tile, d), BF16),
            pltpu.VMEM((tile, d), BF16),
            pltpu.VMEM((2, SUBLANES + tile, d), F32),
        ] + [pltpu.VMEM((2, CARRY, width), BF16)
             for width in (LANES, LANES, 2 * LANES, 2 * LANES) for _ in range(N_KV_HEADS)],
        compiler_params=pltpu.CompilerParams(
            dimension_semantics=("arbitrary", "arbitrary"), vmem_limit_bytes=VMEM_LIMIT_BYTES),
        name="mixer_prompt",
    )(x2, g, w_in, b_in, sink_cols, conv_w, woa, woc, wo)


def _mixer_sample_body(x_ref, ck_ref, cv_ref, st_ref, g_ref, win_ref, bin_ref, sink_ref, cw_ref,
                       woa_ref, woc_ref, wo_ref,
                       xo_ref, kn_ref, vn_ref, un_ref,
                       qs, at, cs, ub, *, n_streams, n_new):
    d = x_ref.shape[1]
    offs = _split_offsets(d)
    keep = ck_ref.shape[1]
    x = x_ref[...]
    h = _rms(x, g_ref[...]).astype(BF16)

    def proj(i):
        return _dot(h, win_ref[:, offs[i]:offs[i + 1]]) + bin_ref[:, offs[i]:offs[i + 1]]

    kn_all = proj(1)
    vn_all = proj(2)
    qs[...] = (proj(0) * ATTN_SCALE).astype(BF16)
    u_all = proj(4) * proj(5)

    pad0 = keep + n_new
    j = lax.broadcasted_iota(jnp.int32, (1, KEY_SPAN), 1)
    bias = jnp.where(j < pad0, 0.0, -jnp.inf).astype(F32)

    for b in range(n_streams):
        r0 = b * n_new
        k_new = kn_all[r0:r0 + n_new, :]
        v_new = vn_all[r0:r0 + n_new, :]
        kn_ref[b, 0:keep - n_new, :] = ck_ref[b, n_new:keep, :]
        kn_ref[b, keep - n_new:keep, :] = k_new
        vn_ref[b, 0:keep - n_new, :] = cv_ref[b, n_new:keep, :]
        vn_ref[b, keep - n_new:keep, :] = v_new
        old = _kv_placements(ck_ref[b], cv_ref[b])
        new = _kv_placements(k_new, v_new)
        ka, kb, va, vb = tuple(
            tuple(jnp.concatenate(
                [o[hh], w[hh], jnp.zeros((KEY_SPAN - pad0, o[hh].shape[1]), BF16)], axis=0)
                for hh in range(N_KV_HEADS))
            for o, w in zip(old, new))
        probs = _attn_probs(qs, r0, n_new, ka, kb, sink_ref, bias)
        _attn_output(at, r0, n_new, probs, va, vb)

        ub[b, 0:SUBLANES, :] = st_ref[b]
        ub[b, SUBLANES:SUBLANES + n_new, :] = u_all[r0:r0 + n_new, :]
        cs[r0:r0 + n_new, :] = _conv_taps(ub.at[b], cw_ref, n_new)
        un_ref[b] = u_all[r0 + n_new - SUBLANES:r0 + n_new, :]

    conv = (proj(3) * cs[...]).astype(BF16)
    merged = (jax.nn.sigmoid(proj(6)) * _dot(at[...], woa_ref[...])
              + jax.nn.sigmoid(proj(7)) * _dot(conv, woc_ref[...]))
    xo_ref[...] = x + _dot(merged.astype(BF16), wo_ref[...])


def _mixer_sample(x2, ck, cv, st, g, w_in, b_in, sink_cols, conv_w, woa, woc, wo, *, layer):
    n, d = x2.shape
    _, n_streams, keep, _ = ck.shape
    n_new = n // n_streams
    assert keep + n_new <= KEY_SPAN and n_new % SUBLANES == 0 and keep >= n_new
    args = (x2, ck, cv, st, g, w_in, b_in, sink_cols, conv_w, woa, woc, wo)

    def whole(a):
        nd = len(a.shape)
        return pl.BlockSpec(a.shape, lambda i: (0,) * nd)

    out_shape = [
        jax.ShapeDtypeStruct((n, d), F32),
        jax.ShapeDtypeStruct((n_streams, keep, KV_DIM), F32),
        jax.ShapeDtypeStruct((n_streams, keep, KV_DIM), F32),
        jax.ShapeDtypeStruct((n_streams, SUBLANES, d), F32),
    ]
    return pl.pallas_call(
        functools.partial(_mixer_sample_body, n_streams=n_streams, n_new=n_new),
        grid=(1,),
        in_specs=[whole(a) if a is x2 or a is sink_cols else _layer_spec(a, layer) for a in args],
        out_specs=[whole(s) for s in out_shape],
        out_shape=out_shape,
        scratch_shapes=[
            pltpu.VMEM((n, d), BF16),
            pltpu.VMEM((n, d), BF16),
            pltpu.VMEM((n, d), F32),
            pltpu.VMEM((n_streams, SUBLANES + n_new, d), F32),
        ],
        compiler_params=pltpu.CompilerParams(
            dimension_semantics=("arbitrary",), vmem_limit_bytes=VMEM_LIMIT_BYTES),
        name="mixer_sample",
    )(*args)


def _sink_columns(sinks_l, nq):
    s = sinks_l.astype(F32).reshape(N_KV_HEADS, PAIRS, 2)
    s = jnp.transpose(s, (0, 2, 1))
    return jnp.repeat(s, nq, axis=2)[..., None]


def kernel(x_prompt, x_sample, cache_k, cache_v, state_conv, norm_ffn1, w1_gate, w1_up, w1_down,
           norm_mix, w_in, b_in, sinks, conv_w, w_o_attn, w_o_conv, w_o, norm_ffn2, w2_gate,
           w2_up, w2_down, norm_final):
    batch, seq, d = x_prompt.shape
    n_streams, n_new, _ = x_sample.shape
    depth = w_in.shape[0]
    keep_s = cache_k.shape[2]
    tile_ffn = 1024
    sub_ffn = 512
    tile_mix = 512

    xp = x_prompt.reshape(batch * seq, d)
    xs = x_sample.reshape(n_streams * n_new, d)
    gf = norm_final.reshape(1, d)
    st_pad = jnp.pad(state_conv, ((0, 0), (0, 0), (SUBLANES - (CONV_WIDTH - 1), 0), (0, 0)))
    ck = cache_k.reshape(depth, n_streams, keep_s, KV_DIM)
    cv = cache_v.reshape(depth, n_streams, keep_s, KV_DIM)

    w1 = (norm_ffn1.reshape(depth, 1, d), w1_gate.astype(BF16), w1_up.astype(BF16),
          w1_down.astype(BF16), gf)
    w2 = (norm_ffn2.reshape(depth, 1, d), w2_gate.astype(BF16), w2_up.astype(BF16),
          w2_down.astype(BF16), gf)
    wm = (norm_mix.reshape(depth, 1, d), w_in.astype(BF16), b_in.reshape(depth, 1, -1))
    wo = (conv_w, w_o_attn.astype(BF16), w_o_conv.astype(BF16), w_o.astype(BF16))

    kp, vp, cp, ks, vs, cs = [], [], [], [], [], []
    for l in range(depth):
        last = l == depth - 1
        xp = _ffn(xp, *w1, layer=l, tile=tile_ffn, sub=sub_ffn, final_norm=False)
        xs = _ffn(xs, *w1, layer=l, tile=xs.shape[0], sub=xs.shape[0], final_norm=False)

        xp, k_l, v_l, u_l = _mixer_prompt(
            xp, *wm, _sink_columns(sinks[l], CHUNK), *wo, layer=l, batch=batch, tile=tile_mix)
        kp.append(k_l.reshape(batch, WINDOW, N_KV_HEADS, HEAD_DIM))
        vp.append(v_l.reshape(batch, WINDOW, N_KV_HEADS, HEAD_DIM))
        cp.append(u_l[:, SUBLANES - (CONV_WIDTH - 1):, :])

        xs, k_l, v_l, u_l = _mixer_sample(
            xs, ck, cv, st_pad, *wm, _sink_columns(sinks[l], n_new), *wo, layer=l)
        ks.append(k_l.reshape(n_streams, keep_s, N_KV_HEADS, HEAD_DIM))
        vs.append(v_l.reshape(n_streams, keep_s, N_KV_HEADS, HEAD_DIM))
        cs.append(u_l[:, SUBLANES - (CONV_WIDTH - 1):, :])

        xp = _ffn(xp, *w2, layer=l, tile=tile_ffn, sub=sub_ffn, final_norm=last)
        xs = _ffn(xs, *w2, layer=l, tile=xs.shape[0], sub=xs.shape[0], final_norm=last)

    return (xp.reshape(batch, seq, d), xs.reshape(n_streams, n_new, d),
            jnp.stack(kp), jnp.stack(vp), jnp.stack(cp),
            jnp.stack(ks), jnp.stack(vs), jnp.stack(cs))
```

```python
import functools

import jax
import jax.numpy as jnp
from jax import lax
from jax.experimental import pallas as pl
from jax.experimental.pallas import tpu as pltpu

F32 = jnp.float32
BF16 = jnp.bfloat16

CHUNK = 64
N_HEADS = 16
N_KV_HEADS = 2
HEAD_DIM = 64
GROUP = N_HEADS // N_KV_HEADS
PAIRS = GROUP // 2
KV_DIM = N_KV_HEADS * HEAD_DIM
WINDOW = 128
CONV_WIDTH = 3
EPS = 1e-6
ATTN_SCALE = HEAD_DIM ** -0.5

KEY_SPAN = 256
LANES = 128
SUBLANES = 8
VMEM_LIMIT_BYTES = 56 * 1024 * 1024


def _rms(x, g):
    y = x * lax.rsqrt(jnp.mean(x * x, axis=-1, keepdims=True) + EPS)
    return y * g


def _dot(a, b):
    return jnp.dot(a, b, preferred_element_type=F32)


def _dot_nt(a, b):
    return lax.dot_general(a, b, (((1,), (1,)), ((), ())), preferred_element_type=F32)


def _const_spec(shape):
    nd = len(shape)
    return pl.BlockSpec(shape, lambda *_: (0,) * nd, pipeline_mode=pl.Buffered(1))


def _layer_spec(stacked, layer):
    rest = stacked.shape[1:]
    return pl.BlockSpec((None,) + rest, lambda *_: (layer,) + (0,) * len(rest),
                        pipeline_mode=pl.Buffered(1))


def _ffn_body(x_ref, g_ref, wg_ref, wu_ref, wd_ref, gf_ref, o_ref, *, final_norm, sub):
    blocks = [pl.ds(r, sub) for r in range(0, x_ref.shape[0], sub)]
    gated = []
    for rows in blocks:
        h = _rms(x_ref[rows, :], g_ref[...]).astype(BF16)
        gated.append((_dot(h, wg_ref[...]), _dot(h, wu_ref[...])))
    for rows, (gate, up) in zip(blocks, gated):
        act = (gate * jax.nn.sigmoid(gate) * up).astype(BF16)
        out = x_ref[rows, :] + 0.5 * _dot(act, wd_ref[...])
        if final_norm:
            out = _rms(out, gf_ref[...])
        o_ref[rows, :] = out


def _ffn(x2, g, wg, wu, wd, gf, *, layer, tile, sub, final_norm):
    n, d = x2.shape
    assert n % tile == 0 and tile % sub == 0
    return pl.pallas_call(
        functools.partial(_ffn_body, final_norm=final_norm, sub=sub),
        grid=(n // tile,),
        in_specs=[
            pl.BlockSpec((tile, d), lambda i: (i, 0)),
            _layer_spec(g, layer),
            _layer_spec(wg, layer),
            _layer_spec(wu, layer),
            _layer_spec(wd, layer),
            _const_spec((1, d)),
        ],
        out_specs=pl.BlockSpec((tile, d), lambda i: (i, 0)),
        out_shape=jax.ShapeDtypeStruct((n, d), F32),
        compiler_params=pltpu.CompilerParams(
            dimension_semantics=("arbitrary",), vmem_limit_bytes=VMEM_LIMIT_BYTES),
        name="ffn_final" if final_norm else "ffn",
    )(x2, g, wg, wu, wd, gf)


def _split_offsets(d):
    sizes = (N_HEADS * HEAD_DIM, KV_DIM, KV_DIM, d, d, d, d, d)
    offs = [0]
    for s in sizes:
        offs.append(offs[-1] + s)
    return offs


def _kv_placements(k2, v2):
    n = k2.shape[0]
    lo = lax.broadcasted_iota(jnp.int32, (n, LANES), 1) < HEAD_DIM
    k2r = pltpu.roll(k2, HEAD_DIM, 1)
    v2r = pltpu.roll(v2, HEAD_DIM, 1)
    ones_lo = jnp.where(lo, 1.0, 0.0)
    ones_hi = jnp.where(lo, 0.0, 1.0)

    def cat(a, b):
        return jnp.concatenate([a, b], axis=1).astype(BF16)

    ka = (jnp.where(lo, k2, 0.0).astype(BF16), jnp.where(lo, k2r, 0.0).astype(BF16))
    kb = (jnp.where(lo, 0.0, k2r).astype(BF16), jnp.where(lo, 0.0, k2).astype(BF16))
    va = (cat(jnp.where(lo, v2, 0.0), ones_lo), cat(jnp.where(lo, v2r, 0.0), ones_lo))
    vb = (cat(jnp.where(lo, 0.0, v2r), ones_hi), cat(jnp.where(lo, 0.0, v2), ones_hi))
    return ka, kb, va, vb


def _attn_probs(qs, q_row0, nq, ka, kb, sink_ref, bias):
    qrows = pl.ds(q_row0, nq)
    lo = lax.broadcasted_iota(jnp.int32, (1, LANES), 1) < HEAD_DIM
    out = []
    for h in range(N_KV_HEADS):
        base = h * GROUP * HEAD_DIM
        q_stack = jnp.concatenate(
            [qs[qrows, base + p * LANES:base + (p + 1) * LANES] for p in range(PAIRS)], axis=0)
        s0 = _dot_nt(q_stack, ka[h]) + bias
        s1 = _dot_nt(q_stack, kb[h]) + bias
        sink0 = sink_ref[h, 0]
        sink1 = sink_ref[h, 1]
        m0 = jnp.maximum(jnp.max(s0, axis=1, keepdims=True), sink0)
        m1 = jnp.maximum(jnp.max(s1, axis=1, keepdims=True), sink1)
        e0 = jnp.exp(s0 - m0).astype(BF16)
        e1 = jnp.exp(s1 - m1).astype(BF16)
        e_sink = jnp.where(lo, jnp.exp(sink0 - m0), jnp.exp(sink1 - m1))
        out.append((e0, e1, e_sink))
    return out


def _attn_output(at, q_row0, nq, probs, va, vb):
    qrows = pl.ds(q_row0, nq)
    for h in range(N_KV_HEADS):
        base = h * GROUP * HEAD_DIM
        e0, e1, e_sink = probs[h]
        o2 = _dot(e0, va[h]) + _dot(e1, vb[h])
        o = (o2[:, 0:LANES] / (o2[:, LANES:2 * LANES] + e_sink)).astype(BF16)
        for p in range(PAIRS):
            at[qrows, base + p * LANES:base + (p + 1) * LANES] = o[p * nq:(p + 1) * nq, :]


ONES_ROWS = 16


def _attn_probs_t(qs, q_row0, nq, keys, sink_ref, bias):
    qrows = pl.ds(q_row0, nq)
    out = []
    for h in range(N_KV_HEADS):
        base = h * GROUP * HEAD_DIM
        q_stack = jnp.concatenate(
            [qs[qrows, base + p * LANES:base + (p + 1) * LANES] for p in range(PAIRS)], axis=0)
        s_both = _dot_nt(keys[h], q_stack)
        per_head = []
        for i in range(2):
            s = s_both[i * KEY_SPAN:(i + 1) * KEY_SPAN, :] + bias
            sink = sink_ref[h, i]
            m = jnp.maximum(jnp.max(s, axis=0, keepdims=True), sink)
            per_head.append((jnp.exp(s - m).astype(BF16), jnp.exp(sink - m)))
        out.append(per_head)
    return out


def _attn_output_t(at, q_row0, nq, probs, values):
    qrows = pl.ds(q_row0, nq)
    for h in range(N_KV_HEADS):
        base = h * GROUP * HEAD_DIM
        halves = []
        for e, e_sink in probs[h]:
            o_t = _dot(values[h], e)
            halves.append(o_t[0:HEAD_DIM, :] / (o_t[HEAD_DIM:HEAD_DIM + 1, :] + e_sink))
        o = jnp.concatenate(halves, axis=0).T.astype(BF16)
        for p in range(PAIRS):
            at[qrows, base + p * LANES:base + (p + 1) * LANES] = o[p * nq:(p + 1) * nq, :]


def _conv_taps(ub, cw_ref, n):
    out = ub[pl.ds(SUBLANES - 2, n), :] * cw_ref[0:1, :]
    out = out + ub[pl.ds(SUBLANES - 1, n), :] * cw_ref[1:2, :]
    return out + ub[pl.ds(SUBLANES, n), :] * cw_ref[2:3, :]


CARRY = KEY_SPAN - CHUNK


def _mixer_prompt_body(x_ref, g_ref, win_ref, bin_ref, sink_ref, cw_ref, woa_ref, woc_ref, wo_ref,
                       xo_ref, kn_ref, vn_ref, un_ref,
                       qs, at, ub, *carry_refs, tile):
    d = x_ref.shape[1]
    offs = _split_offsets(d)
    t = pl.program_id(1)
    kc_refs = carry_refs[0:2 * N_KV_HEADS]
    vc_refs = carry_refs[2 * N_KV_HEADS:]
    cur = lax.rem(t, 2)
    nxt = 1 - cur

    @pl.when(t == 0)
    def _():
        for r in carry_refs:
            r[0] = jnp.zeros(r.shape[1:], r.dtype)
        ub[0, 0:SUBLANES, :] = jnp.zeros((SUBLANES, d), F32)

    x = x_ref[...]
    h = _rms(x, g_ref[...]).astype(BF16)

    def proj(i):
        return _dot(h, win_ref[:, offs[i]:offs[i + 1]]) + bin_ref[:, offs[i]:offs[i + 1]]

    k2 = proj(1)
    v2 = proj(2)
    kn_ref[0] = k2[tile - WINDOW:, :]
    vn_ref[0] = v2[tile - WINDOW:, :]
    qs[...] = (proj(0) * ATTN_SCALE).astype(BF16)

    k_placed = [a for pair in _kv_placements(k2, v2)[0:2] for a in pair]
    k_carried = [r[cur] for r in kc_refs]

    def key_window(i, r0):
        if r0 >= CARRY:
            return k_placed[i][r0 - CARRY:r0 - CARRY + KEY_SPAN, :]
        return jnp.concatenate([k_carried[i][r0:, :], k_placed[i][0:r0 + CHUNK, :]], axis=0)

    v_t = v2.T
    ones = jnp.ones((ONES_ROWS, tile), F32)
    width = KEY_SPAN + tile
    v_bufs = []
    for hh in range(N_KV_HEADS):
        rows = jnp.concatenate([v_t[hh * HEAD_DIM:(hh + 1) * HEAD_DIM, :], ones], axis=0)
        buf_a = jnp.concatenate([vc_refs[hh][cur], rows], axis=1)
        buf_b = pltpu.roll(buf_a, width - CHUNK, 1)
        v_bufs.append((buf_a.astype(BF16), buf_b.astype(BF16)))
        vc_refs[hh][nxt] = buf_a[:, tile:tile + KEY_SPAN]

    def value_window(hh, c):
        buf_a, buf_b = v_bufs[hh]
        if c % 2:
            return buf_a[:, (c + 1) * CHUNK:(c + 1) * CHUNK + KEY_SPAN]
        return buf_b[:, c * CHUNK:c * CHUNK + KEY_SPAN]

    j = lax.broadcasted_iota(jnp.int32, (KEY_SPAN, KEY_SPAN), 0)
    probs = []
    for c in range(tile // CHUNK):
        r0 = c * CHUNK
        pos = t * tile + (r0 - CARRY) + j
        valid = jnp.logical_and(j >= KEY_SPAN - WINDOW - CHUNK, pos >= 0)
        bias = jnp.where(valid, 0.0, -jnp.inf).astype(F32)
        keys = [jnp.concatenate([key_window(hh, r0), key_window(N_KV_HEADS + hh, r0)], axis=0)
                for hh in range(N_KV_HEADS)]
        probs.append(_attn_probs_t(qs, r0, CHUNK, keys, sink_ref, bias))

    u = proj(4) * proj(5)
    ub[cur, SUBLANES:SUBLANES + tile, :] = u
    conv = (proj(3) * _conv_taps(ub.at[cur], cw_ref, tile)).astype(BF16)
    un_ref[0] = u[tile - SUBLANES:, :]
    ub[nxt, 0:SUBLANES, :] = u[tile - SUBLANES:, :]
    gate_attn = jax.nn.sigmoid(proj(6))
    gate_conv = jax.nn.sigmoid(proj(7))
    y_conv = gate_conv * _dot(conv, woc_ref[...])

    for c in range(tile // CHUNK):
        values = [value_window(hh, c) for hh in range(N_KV_HEADS)]
        _attn_output_t(at, c * CHUNK, CHUNK, probs[c], values)

    merged = gate_attn * _dot(at[...], woa_ref[...]) + y_conv
    xo_ref[...] = x + _dot(merged.astype(BF16), wo_ref[...])

    for r, a in zip(kc_refs, k_placed):
        r[nxt] = a[tile - CARRY:, :]


def _mixer_prompt(x2, g, w_in, b_in, sink_cols, conv_w, woa, woc, wo, *, layer, batch, tile):
    n, d = x2.shape
    seq = n // batch
    assert seq % tile == 0 and tile % CHUNK == 0 and tile >= CARRY and tile >= WINDOW
    nt = seq // tile
    return pl.pallas_call(
        functools.partial(_mixer_prompt_body, tile=tile),
        grid=(batch, nt),
        in_specs=[
            pl.BlockSpec((tile, d), lambda b, t: (b * nt + t, 0)),
            _layer_spec(g, layer),
            _layer_spec(w_in, layer),
            _layer_spec(b_in, layer),
            _const_spec(sink_cols.shape),
            _layer_spec(conv_w, layer),
            _layer_spec(woa, layer),
            _layer_spec(woc, layer),
            _layer_spec(wo, layer),
        ],
        out_specs=[
            pl.BlockSpec((tile, d), lambda b, t: (b * nt + t, 0)),
            pl.BlockSpec((1, WINDOW, KV_DIM), lambda b, t: (b, 0, 0)),
            pl.BlockSpec((1, WINDOW, KV_DIM), lambda b, t: (b, 0, 0)),
            pl.BlockSpec((1, SUBLANES, d), lambda b, t: (b, 0, 0)),
        ],
        out_shape=[
            jax.ShapeDtypeStruct((n, d), F32),
            jax.ShapeDtypeStruct((batch, WINDOW, KV_DIM), F32),
            jax.ShapeDtypeStruct((batch, WINDOW, KV_DIM), F32),
            jax.ShapeDtypeStruct((batch, SUBLANES, d), F32),
        ],
        scratch_shapes=[
            pltpu.VMEM((tile, d), BF16),
            pltpu.VMEM((tile, d), BF16),
            pltpu.VMEM((2, SUBLANES + tile, d), F32),
        ] + [pltpu.VMEM((2, CARRY, LANES), BF16) for _ in range(2 * N_KV_HEADS)]
          + [pltpu.VMEM((2, HEAD_DIM + ONES_ROWS, KEY_SPAN), F32) for _ in range(N_KV_HEADS)],
        compiler_params=pltpu.CompilerParams(
            dimension_semantics=("arbitrary", "arbitrary"), vmem_limit_bytes=VMEM_LIMIT_BYTES),
        name="mixer_prompt",
    )(x2, g, w_in, b_in, sink_cols, conv_w, woa, woc, wo)


def _mixer_sample_body(x_ref, ck_ref, cv_ref, st_ref, g_ref, win_ref, bin_ref, sink_ref, cw_ref,
                       woa_ref, woc_ref, wo_ref,
                       xo_ref, kn_ref, vn_ref, un_ref,
                       qs, at, cs, ub, *, n_streams, n_new):
    d = x_ref.shape[1]
    offs = _split_offsets(d)
    keep = ck_ref.shape[1]
    x = x_ref[...]
    h = _rms(x, g_ref[...]).astype(BF16)

    def proj(i):
        return _dot(h, win_ref[:, offs[i]:offs[i + 1]]) + bin_ref[:, offs[i]:offs[i + 1]]

    kn_all = proj(1)
    vn_all = proj(2)
    qs[...] = (proj(0) * ATTN_SCALE).astype(BF16)
    u_all = proj(4) * proj(5)

    pad0 = keep + n_new
    j = lax.broadcasted_iota(jnp.int32, (1, KEY_SPAN), 1)
    bias = jnp.where(j < pad0, 0.0, -jnp.inf).astype(F32)

    for b in range(n_streams):
        r0 = b * n_new
        k_new = kn_all[r0:r0 + n_new, :]
        v_new = vn_all[r0:r0 + n_new, :]
        kn_ref[b, 0:keep - n_new, :] = ck_ref[b, n_new:keep, :]
        kn_ref[b, keep - n_new:keep, :] = k_new
        vn_ref[b, 0:keep - n_new, :] = cv_ref[b, n_new:keep, :]
        vn_ref[b, keep - n_new:keep, :] = v_new
        old = _kv_placements(ck_ref[b], cv_ref[b])
        new = _kv_placements(k_new, v_new)
        ka, kb, va, vb = tuple(
            tuple(jnp.concatenate(
                [o[hh], w[hh], jnp.zeros((KEY_SPAN - pad0, o[hh].shape[1]), BF16)], axis=0)
                for hh in range(N_KV_HEADS))
            for o, w in zip(old, new))
        probs = _attn_probs(qs, r0, n_new, ka, kb, sink_ref, bias)
        _attn_output(at, r0, n_new, probs, va, vb)

        ub[b, 0:SUBLANES, :] = st_ref[b]
        ub[b, SUBLANES:SUBLANES + n_new, :] = u_all[r0:r0 + n_new, :]
        cs[r0:r0 + n_new, :] = _conv_taps(ub.at[b], cw_ref, n_new)
        un_ref[b] = u_all[r0 + n_new - SUBLANES:r0 + n_new, :]

    conv = (proj(3) * cs[...]).astype(BF16)
    merged = (jax.nn.sigmoid(proj(6)) * _dot(at[...], woa_ref[...])
              + jax.nn.sigmoid(proj(7)) * _dot(conv, woc_ref[...]))
    xo_ref[...] = x + _dot(merged.astype(BF16), wo_ref[...])


def _mixer_sample(x2, ck, cv, st, g, w_in, b_in, sink_cols, conv_w, woa, woc, wo, *, layer):
    n, d = x2.shape
    _, n_streams, keep, _ = ck.shape
    n_new = n // n_streams
    assert keep + n_new <= KEY_SPAN and n_new % SUBLANES == 0 and keep >= n_new
    args = (x2, ck, cv, st, g, w_in, b_in, sink_cols, conv_w, woa, woc, wo)

    def whole(a):
        nd = len(a.shape)
        return pl.BlockSpec(a.shape, lambda i: (0,) * nd)

    out_shape = [
        jax.ShapeDtypeStruct((n, d), F32),
        jax.ShapeDtypeStruct((n_streams, keep, KV_DIM), F32),
        jax.ShapeDtypeStruct((n_streams, keep, KV_DIM), F32),
        jax.ShapeDtypeStruct((n_streams, SUBLANES, d), F32),
    ]
    return pl.pallas_call(
        functools.partial(_mixer_sample_body, n_streams=n_streams, n_new=n_new),
        grid=(1,),
        in_specs=[whole(a) if a is x2 or a is sink_cols else _layer_spec(a, layer) for a in args],
        out_specs=[whole(s) for s in out_shape],
        out_shape=out_shape,
        scratch_shapes=[
            pltpu.VMEM((n, d), BF16),
            pltpu.VMEM((n, d), BF16),
            pltpu.VMEM((n, d), F32),
            pltpu.VMEM((n_streams, SUBLANES + n_new, d), F32),
        ],
        compiler_params=pltpu.CompilerParams(
            dimension_semantics=("arbitrary",), vmem_limit_bytes=VMEM_LIMIT_BYTES),
        name="mixer_sample",
    )(*args)


def _sink_columns(sinks_l, nq):
    s = sinks_l.astype(F32).reshape(N_KV_HEADS, PAIRS, 2)
    s = jnp.transpose(s, (0, 2, 1))
    return jnp.repeat(s, nq, axis=2)[..., None]


def _sink_rows(sinks_l, nq):
    return jnp.swapaxes(_sink_columns(sinks_l, nq), 2, 3)


def kernel(x_prompt, x_sample, cache_k, cache_v, state_conv, norm_ffn1, w1_gate, w1_up, w1_down,
           norm_mix, w_in, b_in, sinks, conv_w, w_o_attn, w_o_conv, w_o, norm_ffn2, w2_gate,
           w2_up, w2_down, norm_final):
    batch, seq, d = x_prompt.shape
    n_streams, n_new, _ = x_sample.shape
    depth = w_in.shape[0]
    keep_s = cache_k.shape[2]
    tile_ffn = 1024
    sub_ffn = 512
    tile_mix = 512

    xp = x_prompt.reshape(batch * seq, d)
    xs = x_sample.reshape(n_streams * n_new, d)
    gf = norm_final.reshape(1, d)
    st_pad = jnp.pad(state_conv, ((0, 0), (0, 0), (SUBLANES - (CONV_WIDTH - 1), 0), (0, 0)))
    ck = cache_k.reshape(depth, n_streams, keep_s, KV_DIM)
    cv = cache_v.reshape(depth, n_streams, keep_s, KV_DIM)

    w1 = (norm_ffn1.reshape(depth, 1, d), w1_gate.astype(BF16), w1_up.astype(BF16),
          w1_down.astype(BF16), gf)
    w2 = (norm_ffn2.reshape(depth, 1, d), w2_gate.astype(BF16), w2_up.astype(BF16),
          w2_down.astype(BF16), gf)
    wm = (norm_mix.reshape(depth, 1, d), w_in.astype(BF16), b_in.reshape(depth, 1, -1))
    wo = (conv_w, w_o_attn.astype(BF16), w_o_conv.astype(BF16), w_o.astype(BF16))

    kp, vp, cp, ks, vs, cs = [], [], [], [], [], []
    for l in range(depth):
        last = l == depth - 1
        xp = _ffn(xp, *w1, layer=l, tile=tile_ffn, sub=sub_ffn, final_norm=False)
        xs = _ffn(xs, *w1, layer=l, tile=xs.shape[0], sub=xs.shape[0], final_norm=False)

        xp, k_l, v_l, u_l = _mixer_prompt(
            xp, *wm, _sink_rows(sinks[l], CHUNK), *wo, layer=l, batch=batch, tile=tile_mix)
        kp.append(k_l.reshape(batch, WINDOW, N_KV_HEADS, HEAD_DIM))
        vp.append(v_l.reshape(batch, WINDOW, N_KV_HEADS, HEAD_DIM))
        cp.append(u_l[:, SUBLANES - (CONV_WIDTH - 1):, :])

        xs, k_l, v_l, u_l = _mixer_sample(
            xs, ck, cv, st_pad, *wm, _sink_columns(sinks[l], n_new), *wo, layer=l)
        ks.append(k_l.reshape(n_streams, keep_s, N_KV_HEADS, HEAD_DIM))
        vs.append(v_l.reshape(n_streams, keep_s, N_KV_HEADS, HEAD_DIM))
        cs.append(u_l[:, SUBLANES - (CONV_WIDTH - 1):, :])

        xp = _ffn(xp, *w2, layer=l, tile=tile_ffn, sub=sub_ffn, final_norm=last)
        xs = _ffn(xs, *w2, layer=l, tile=xs.shape[0], sub=xs.shape[0], final_norm=last)

    return (xp.reshape(batch, seq, d), xs.reshape(n_streams, n_new, d),
            jnp.stack(kp), jnp.stack(vp), jnp.stack(cp),
            jnp.stack(ks), jnp.stack(vs), jnp.stack(cs))
```

```python
import functools

import jax
import jax.numpy as jnp
from jax import lax
from jax.experimental import pallas as pl
from jax.experimental.pallas import tpu as pltpu

F32 = jnp.float32
BF16 = jnp.bfloat16

CHUNK = 64
N_HEADS = 16
N_KV_HEADS = 2
HEAD_DIM = 64
GROUP = N_HEADS // N_KV_HEADS
PAIRS = GROUP // 2
KV_DIM = N_KV_HEADS * HEAD_DIM
WINDOW = 128
CONV_WIDTH = 3
EPS = 1e-6
ATTN_SCALE = HEAD_DIM ** -0.5

KEY_SPAN = 256
LANES = 128
SUBLANES = 8
VMEM_LIMIT_BYTES = 56 * 1024 * 1024


def _rms(x, g):
    y = x * lax.rsqrt(jnp.mean(x * x, axis=-1, keepdims=True) + EPS)
    return y * g


def _dot(a, b):
    return jnp.dot(a, b, preferred_element_type=F32)


def _dot_nt(a, b):
    return lax.dot_general(a, b, (((1,), (1,)), ((), ())), preferred_element_type=F32)


def _const_spec(shape):
    nd = len(shape)
    return pl.BlockSpec(shape, lambda *_: (0,) * nd, pipeline_mode=pl.Buffered(1))


def _layer_spec(stacked, layer):
    rest = stacked.shape[1:]
    return pl.BlockSpec((None,) + rest, lambda *_: (layer,) + (0,) * len(rest),
                        pipeline_mode=pl.Buffered(1))


def _ffn_body(x_ref, g_ref, wg_ref, wu_ref, wd_ref, gf_ref, o_ref, *, final_norm, sub):
    blocks = [pl.ds(r, sub) for r in range(0, x_ref.shape[0], sub)]
    gated = []
    for rows in blocks:
        h = _rms(x_ref[rows, :], g_ref[...]).astype(BF16)
        gated.append((_dot(h, wg_ref[...]), _dot(h, wu_ref[...])))
    for rows, (gate, up) in zip(blocks, gated):
        act = (gate * jax.nn.sigmoid(gate) * up).astype(BF16)
        out = x_ref[rows, :] + 0.5 * _dot(act, wd_ref[...])
        if final_norm:
            out = _rms(out, gf_ref[...])
        o_ref[rows, :] = out


def _ffn(x2, g, wg, wu, wd, gf, *, layer, tile, sub, final_norm):
    n, d = x2.shape
    assert n % tile == 0 and tile % sub == 0
    return pl.pallas_call(
        functools.partial(_ffn_body, final_norm=final_norm, sub=sub),
        grid=(n // tile,),
        in_specs=[
            pl.BlockSpec((tile, d), lambda i: (i, 0)),
            _layer_spec(g, layer),
            _layer_spec(wg, layer),
            _layer_spec(wu, layer),
            _layer_spec(wd, layer),
            _const_spec((1, d)),
        ],
        out_specs=pl.BlockSpec((tile, d), lambda i: (i, 0)),
        out_shape=jax.ShapeDtypeStruct((n, d), F32),
        compiler_params=pltpu.CompilerParams(
            dimension_semantics=("arbitrary",), vmem_limit_bytes=VMEM_LIMIT_BYTES),
        name="ffn_final" if final_norm else "ffn",
    )(x2, g, wg, wu, wd, gf)


def _split_offsets(d):
    sizes = (N_HEADS * HEAD_DIM, KV_DIM, KV_DIM, d, d, d, d, d)
    offs = [0]
    for s in sizes:
        offs.append(offs[-1] + s)
    return offs


def _kv_placements(k2, v2):
    n = k2.shape[0]
    lo = lax.broadcasted_iota(jnp.int32, (n, LANES), 1) < HEAD_DIM
    k2r = pltpu.roll(k2, HEAD_DIM, 1)
    v2r = pltpu.roll(v2, HEAD_DIM, 1)
    ones_lo = jnp.where(lo, 1.0, 0.0)
    ones_hi = jnp.where(lo, 0.0, 1.0)

    def cat(a, b):
        return jnp.concatenate([a, b], axis=1).astype(BF16)

    ka = (jnp.where(lo, k2, 0.0).astype(BF16), jnp.where(lo, k2r, 0.0).astype(BF16))
    kb = (jnp.where(lo, 0.0, k2r).astype(BF16), jnp.where(lo, 0.0, k2).astype(BF16))
    va = (cat(jnp.where(lo, v2, 0.0), ones_lo), cat(jnp.where(lo, v2r, 0.0), ones_lo))
    vb = (cat(jnp.where(lo, 0.0, v2r), ones_hi), cat(jnp.where(lo, 0.0, v2), ones_hi))
    return ka, kb, va, vb


def _attn_probs(qs, q_row0, nq, ka, kb, sink_ref, bias):
    qrows = pl.ds(q_row0, nq)
    lo = lax.broadcasted_iota(jnp.int32, (1, LANES), 1) < HEAD_DIM
    out = []
    for h in range(N_KV_HEADS):
        base = h * GROUP * HEAD_DIM
        q_stack = jnp.concatenate(
            [qs[qrows, base + p * LANES:base + (p + 1) * LANES] for p in range(PAIRS)], axis=0)
        s0 = _dot_nt(q_stack, ka[h]) + bias
        s1 = _dot_nt(q_stack, kb[h]) + bias
        sink0 = sink_ref[h, 0]
        sink1 = sink_ref[h, 1]
        m0 = jnp.maximum(jnp.max(s0, axis=1, keepdims=True), sink0)
        m1 = jnp.maximum(jnp.max(s1, axis=1, keepdims=True), sink1)
        e0 = jnp.exp(s0 - m0).astype(BF16)
        e1 = jnp.exp(s1 - m1).astype(BF16)
        e_sink = jnp.where(lo, jnp.exp(sink0 - m0), jnp.exp(sink1 - m1))
        out.append((e0, e1, e_sink))
    return out


def _attn_output(at, q_row0, nq, probs, va, vb):
    qrows = pl.ds(q_row0, nq)
    for h in range(N_KV_HEADS):
        base = h * GROUP * HEAD_DIM
        e0, e1, e_sink = probs[h]
        o2 = _dot(e0, va[h]) + _dot(e1, vb[h])
        o = (o2[:, 0:LANES] / (o2[:, LANES:2 * LANES] + e_sink)).astype(BF16)
        for p in range(PAIRS):
            at[qrows, base + p * LANES:base + (p + 1) * LANES] = o[p * nq:(p + 1) * nq, :]


ONES_ROWS = 16


def _attn_probs_t(qs, q_row0, nq, keys, sink_ref, bias):
    qrows = pl.ds(q_row0, nq)
    out = []
    for h in range(N_KV_HEADS):
        base = h * GROUP * HEAD_DIM
        q_stack = jnp.concatenate(
            [qs[qrows, base + p * LANES:base + (p + 1) * LANES] for p in range(PAIRS)], axis=0)
        s_both = _dot_nt(keys[h], q_stack)
        n_keys = keys[h].shape[0] // 2
        per_head = []
        for i in range(2):
            s = s_both[i * n_keys:(i + 1) * n_keys, :] + bias
            sink = sink_ref[h, i]
            m = jnp.maximum(jnp.max(s, axis=0, keepdims=True), sink)
            e = jnp.exp(s - m).astype(BF16)
            if n_keys < KEY_SPAN:
                e = jnp.concatenate([jnp.zeros((KEY_SPAN - n_keys, e.shape[1]), BF16), e], axis=0)
            per_head.append((e, jnp.exp(sink - m)))
        out.append(per_head)
    return out


def _attn_output_t(at, q_row0, nq, probs, values):
    qrows = pl.ds(q_row0, nq)
    for h in range(N_KV_HEADS):
        base = h * GROUP * HEAD_DIM
        halves = []
        for e, e_sink in probs[h]:
            o_t = _dot(values[h], e)
            halves.append(o_t[0:HEAD_DIM, :] / (o_t[HEAD_DIM:HEAD_DIM + 1, :] + e_sink))
        o = jnp.concatenate(halves, axis=0).T.astype(BF16)
        for p in range(PAIRS):
            at[qrows, base + p * LANES:base + (p + 1) * LANES] = o[p * nq:(p + 1) * nq, :]


def _conv_taps(ub, cw_ref, n):
    out = ub[pl.ds(SUBLANES - 2, n), :] * cw_ref[0:1, :]
    out = out + ub[pl.ds(SUBLANES - 1, n), :] * cw_ref[1:2, :]
    return out + ub[pl.ds(SUBLANES, n), :] * cw_ref[2:3, :]


VISIBLE = WINDOW + CHUNK


def _mixer_prompt_body(x_ref, g_ref, win_ref, bin_ref, sink_ref, cw_ref, woa_ref, woc_ref, wo_ref,
                       xo_ref, kn_ref, vn_ref, un_ref,
                       qs, at, ub, *carry_refs, tile):
    d = x_ref.shape[1]
    offs = _split_offsets(d)
    t = pl.program_id(1)
    kc_refs = carry_refs[0:2 * N_KV_HEADS]
    vc_refs = carry_refs[2 * N_KV_HEADS:]
    cur = lax.rem(t, 2)
    nxt = 1 - cur

    @pl.when(t == 0)
    def _():
        for r in carry_refs:
            r[0] = jnp.zeros(r.shape[1:], r.dtype)
        ub[0, 0:SUBLANES, :] = jnp.zeros((SUBLANES, d), F32)

    x = x_ref[...]
    h = _rms(x, g_ref[...]).astype(BF16)

    def proj(i):
        return _dot(h, win_ref[:, offs[i]:offs[i + 1]]) + bin_ref[:, offs[i]:offs[i + 1]]

    k2 = proj(1)
    v2 = proj(2)
    kn_ref[0] = k2[tile - WINDOW:, :]
    vn_ref[0] = v2[tile - WINDOW:, :]
    qs[...] = (proj(0) * ATTN_SCALE).astype(BF16)

    k_placed = [a for pair in _kv_placements(k2, v2)[0:2] for a in pair]
    k_carried = [r[cur] for r in kc_refs]

    def key_window(i, r0):
        if r0 >= WINDOW:
            return k_placed[i][r0 - WINDOW:r0 - WINDOW + VISIBLE, :]
        return jnp.concatenate([k_carried[i][r0:, :], k_placed[i][0:r0 + CHUNK, :]], axis=0)

    v_t = v2.T
    ones = jnp.ones((ONES_ROWS, tile), F32)
    width = KEY_SPAN + tile
    v_bufs = []
    for hh in range(N_KV_HEADS):
        rows = jnp.concatenate([v_t[hh * HEAD_DIM:(hh + 1) * HEAD_DIM, :], ones], axis=0)
        buf_a = jnp.concatenate([vc_refs[hh][cur], rows], axis=1)
        buf_b = pltpu.roll(buf_a, width - CHUNK, 1)
        v_bufs.append((buf_a.astype(BF16), buf_b.astype(BF16)))
        vc_refs[hh][nxt] = buf_a[:, tile:tile + KEY_SPAN]

    def value_window(hh, c):
        buf_a, buf_b = v_bufs[hh]
        if c % 2:
            return buf_a[:, (c + 1) * CHUNK:(c + 1) * CHUNK + KEY_SPAN]
        return buf_b[:, c * CHUNK:c * CHUNK + KEY_SPAN]

    j = lax.broadcasted_iota(jnp.int32, (VISIBLE, PAIRS * CHUNK), 0)
    probs = []
    for c in range(tile // CHUNK):
        r0 = c * CHUNK
        pos = t * tile + (r0 - WINDOW) + j
        bias = jnp.where(pos >= 0, 0.0, -jnp.inf).astype(F32)
        keys = [jnp.concatenate([key_window(hh, r0), key_window(N_KV_HEADS + hh, r0)], axis=0)
                for hh in range(N_KV_HEADS)]
        probs.append(_attn_probs_t(qs, r0, CHUNK, keys, sink_ref, bias))

    u = proj(4) * proj(5)
    ub[cur, SUBLANES:SUBLANES + tile, :] = u
    conv = (proj(3) * _conv_taps(ub.at[cur], cw_ref, tile)).astype(BF16)
    un_ref[0] = u[tile - SUBLANES:, :]
    ub[nxt, 0:SUBLANES, :] = u[tile - SUBLANES:, :]
    gate_attn = jax.nn.sigmoid(proj(6))
    gate_conv = jax.nn.sigmoid(proj(7))
    y_conv = gate_conv * _dot(conv, woc_ref[...])

    for c in range(tile // CHUNK):
        values = [value_window(hh, c) for hh in range(N_KV_HEADS)]
        _attn_output_t(at, c * CHUNK, CHUNK, probs[c], values)

    merged = gate_attn * _dot(at[...], woa_ref[...]) + y_conv
    xo_ref[...] = x + _dot(merged.astype(BF16), wo_ref[...])

    for r, a in zip(kc_refs, k_placed):
        r[nxt] = a[tile - WINDOW:, :]


def _mixer_prompt(x2, g, w_in, b_in, sink_cols, conv_w, woa, woc, wo, *, layer, batch, tile):
    n, d = x2.shape
    seq = n // batch
    assert seq % tile == 0 and tile % (2 * CHUNK) == 0 and tile >= KEY_SPAN
    nt = seq // tile
    return pl.pallas_call(
        functools.partial(_mixer_prompt_body, tile=tile),
        grid=(batch, nt),
        in_specs=[
            pl.BlockSpec((tile, d), lambda b, t: (b * nt + t, 0)),
            _layer_spec(g, layer),
            _layer_spec(w_in, layer),
            _layer_spec(b_in, layer),
            _const_spec(sink_cols.shape),
            _layer_spec(conv_w, layer),
            _layer_spec(woa, layer),
            _layer_spec(woc, layer),
            _layer_spec(wo, layer),
        ],
        out_specs=[
            pl.BlockSpec((tile, d), lambda b, t: (b * nt + t, 0)),
            pl.BlockSpec((1, WINDOW, KV_DIM), lambda b, t: (b, 0, 0)),
            pl.BlockSpec((1, WINDOW, KV_DIM), lambda b, t: (b, 0, 0)),
            pl.BlockSpec((1, SUBLANES, d), lambda b, t: (b, 0, 0)),
        ],
        out_shape=[
            jax.ShapeDtypeStruct((n, d), F32),
            jax.ShapeDtypeStruct((batch, WINDOW, KV_DIM), F32),
            jax.ShapeDtypeStruct((batch, WINDOW, KV_DIM), F32),
            jax.ShapeDtypeStruct((batch, SUBLANES, d), F32),
        ],
        scratch_shapes=[
            pltpu.VMEM((tile, d), BF16),
            pltpu.VMEM((tile, d), BF16),
            pltpu.VMEM((2, SUBLANES + tile, d), F32),
        ] + [pltpu.VMEM((2, WINDOW, LANES), BF16) for _ in range(2 * N_KV_HEADS)]
          + [pltpu.VMEM((2, HEAD_DIM + ONES_ROWS, KEY_SPAN), F32) for _ in range(N_KV_HEADS)],
        compiler_params=pltpu.CompilerParams(
            dimension_semantics=("arbitrary", "arbitrary"), vmem_limit_bytes=VMEM_LIMIT_BYTES),
        name="mixer_prompt",
    )(x2, g, w_in, b_in, sink_cols, conv_w, woa, woc, wo)


def _mixer_sample_body(x_ref, ck_ref, cv_ref, st_ref, g_ref, win_ref, bin_ref, sink_ref, cw_ref,
                       woa_ref, woc_ref, wo_ref,
                       xo_ref, kn_ref, vn_ref, un_ref,
                       qs, at, cs, ub, *, n_streams, n_new):
    d = x_ref.shape[1]
    offs = _split_offsets(d)
    keep = ck_ref.shape[1]
    x = x_ref[...]
    h = _rms(x, g_ref[...]).astype(BF16)

    def proj(i):
        return _dot(h, win_ref[:, offs[i]:offs[i + 1]]) + bin_ref[:, offs[i]:offs[i + 1]]

    kn_all = proj(1)
    vn_all = proj(2)
    qs[...] = (proj(0) * ATTN_SCALE).astype(BF16)
    u_all = proj(4) * proj(5)

    pad0 = keep + n_new
    j = lax.broadcasted_iota(jnp.int32, (1, KEY_SPAN), 1)
    bias = jnp.where(j < pad0, 0.0, -jnp.inf).astype(F32)

    for b in range(n_streams):
        r0 = b * n_new
        k_new = kn_all[r0:r0 + n_new, :]
        v_new = vn_all[r0:r0 + n_new, :]
        kn_ref[b, 0:keep - n_new, :] = ck_ref[b, n_new:keep, :]
        kn_ref[b, keep - n_new:keep, :] = k_new
        vn_ref[b, 0:keep - n_new, :] = cv_ref[b, n_new:keep, :]
        vn_ref[b, keep - n_new:keep, :] = v_new
        old = _kv_placements(ck_ref[b], cv_ref[b])
        new = _kv_placements(k_new, v_new)
        ka, kb, va, vb = tuple(
            tuple(jnp.concatenate(
                [o[hh], w[hh], jnp.zeros((KEY_SPAN - pad0, o[hh].shape[1]), BF16)], axis=0)
                for hh in range(N_KV_HEADS))
            for o, w in zip(old, new))
        probs = _attn_probs(qs, r0, n_new, ka, kb, sink_ref, bias)
        _attn_output(at, r0, n_new, probs, va, vb)

        ub[b, 0:SUBLANES, :] = st_ref[b]
        ub[b, SUBLANES:SUBLANES + n_new, :] = u_all[r0:r0 + n_new, :]
        cs[r0:r0 + n_new, :] = _conv_taps(ub.at[b], cw_ref, n_new)
        un_ref[b] = u_all[r0 + n_new - SUBLANES:r0 + n_new, :]

    conv = (proj(3) * cs[...]).astype(BF16)
    merged = (jax.nn.sigmoid(proj(6)) * _dot(at[...], woa_ref[...])
              + jax.nn.sigmoid(proj(7)) * _dot(conv, woc_ref[...]))
    xo_ref[...] = x + _dot(merged.astype(BF16), wo_ref[...])


def _mixer_sample(x2, ck, cv, st, g, w_in, b_in, sink_cols, conv_w, woa, woc, wo, *, layer):
    n, d = x2.shape
    _, n_streams, keep, _ = ck.shape
    n_new = n // n_streams
    assert keep + n_new <= KEY_SPAN and n_new % SUBLANES == 0 and keep >= n_new
    args = (x2, ck, cv, st, g, w_in, b_in, sink_cols, conv_w, woa, woc, wo)

    def whole(a):
        nd = len(a.shape)
        return pl.BlockSpec(a.shape, lambda i: (0,) * nd)

    out_shape = [
        jax.ShapeDtypeStruct((n, d), F32),
        jax.ShapeDtypeStruct((n_streams, keep, KV_DIM), F32),
        jax.ShapeDtypeStruct((n_streams, keep, KV_DIM), F32),
        jax.ShapeDtypeStruct((n_streams, SUBLANES, d), F32),
    ]
    return pl.pallas_call(
        functools.partial(_mixer_sample_body, n_streams=n_streams, n_new=n_new),
        grid=(1,),
        in_specs=[whole(a) if a is x2 or a is sink_cols else _layer_spec(a, layer) for a in args],
        out_specs=[whole(s) for s in out_shape],
        out_shape=out_shape,
        scratch_shapes=[
            pltpu.VMEM((n, d), BF16),
            pltpu.VMEM((n, d), BF16),
            pltpu.VMEM((n, d), F32),
            pltpu.VMEM((n_streams, SUBLANES + n_new, d), F32),
        ],
        compiler_params=pltpu.CompilerParams(
            dimension_semantics=("arbitrary",), vmem_limit_bytes=VMEM_LIMIT_BYTES),
        name="mixer_sample",
    )(*args)


def _sink_columns(sinks_l, nq):
    s = sinks_l.astype(F32).reshape(N_KV_HEADS, PAIRS, 2)
    s = jnp.transpose(s, (0, 2, 1))
    return jnp.repeat(s, nq, axis=2)[..., None]


def _sink_rows(sinks_l, nq):
    return jnp.swapaxes(_sink_columns(sinks_l, nq), 2, 3)


def kernel(x_prompt, x_sample, cache_k, cache_v, state_conv, norm_ffn1, w1_gate, w1_up, w1_down,
           norm_mix, w_in, b_in, sinks, conv_w, w_o_attn, w_o_conv, w_o, norm_ffn2, w2_gate,
           w2_up, w2_down, norm_final):
    batch, seq, d = x_prompt.shape
    n_streams, n_new, _ = x_sample.shape
    depth = w_in.shape[0]
    keep_s = cache_k.shape[2]
    tile_ffn = 1024
    sub_ffn = 512
    tile_mix = 512

    xp = x_prompt.reshape(batch * seq, d)
    xs = x_sample.reshape(n_streams * n_new, d)
    gf = norm_final.reshape(1, d)
    st_pad = jnp.pad(state_conv, ((0, 0), (0, 0), (SUBLANES - (CONV_WIDTH - 1), 0), (0, 0)))
    ck = cache_k.reshape(depth, n_streams, keep_s, KV_DIM)
    cv = cache_v.reshape(depth, n_streams, keep_s, KV_DIM)

    w1 = (norm_ffn1.reshape(depth, 1, d), w1_gate.astype(BF16), w1_up.astype(BF16),
          w1_down.astype(BF16), gf)
    w2 = (norm_ffn2.reshape(depth, 1, d), w2_gate.astype(BF16), w2_up.astype(BF16),
          w2_down.astype(BF16), gf)
    wm = (norm_mix.reshape(depth, 1, d), w_in.astype(BF16), b_in.reshape(depth, 1, -1))
    wo = (conv_w, w_o_attn.astype(BF16), w_o_conv.astype(BF16), w_o.astype(BF16))

    kp, vp, cp, ks, vs, cs = [], [], [], [], [], []
    for l in range(depth):
        last = l == depth - 1
        xp = _ffn(xp, *w1, layer=l, tile=tile_ffn, sub=sub_ffn, final_norm=False)
        xs = _ffn(xs, *w1, layer=l, tile=xs.shape[0], sub=xs.shape[0], final_norm=False)

        xp, k_l, v_l, u_l = _mixer_prompt(
            xp, *wm, _sink_rows(sinks[l], CHUNK), *wo, layer=l, batch=batch, tile=tile_mix)
        kp.append(k_l.reshape(batch, WINDOW, N_KV_HEADS, HEAD_DIM))
        vp.append(v_l.reshape(batch, WINDOW, N_KV_HEADS, HEAD_DIM))
        cp.append(u_l[:, SUBLANES - (CONV_WIDTH - 1):, :])

        xs, k_l, v_l, u_l = _mixer_sample(
            xs, ck, cv, st_pad, *wm, _sink_columns(sinks[l], n_new), *wo, layer=l)
        ks.append(k_l.reshape(n_streams, keep_s, N_KV_HEADS, HEAD_DIM))
        vs.append(v_l.reshape(n_streams, keep_s, N_KV_HEADS, HEAD_DIM))
        cs.append(u_l[:, SUBLANES - (CONV_WIDTH - 1):, :])

        xp = _ffn(xp, *w2, layer=l, tile=tile_ffn, sub=sub_ffn, final_norm=last)
        xs = _ffn(xs, *w2, layer=l, tile=xs.shape[0], sub=xs.shape[0], final_norm=last)

    return (xp.reshape(batch, seq, d), xs.reshape(n_streams, n_new, d),
            jnp.stack(kp), jnp.stack(vp), jnp.stack(cp),
            jnp.stack(ks), jnp.stack(vs), jnp.stack(cs))
```

```python
import functools

import jax
import jax.numpy as jnp
from jax import lax
from jax.experimental import pallas as pl
from jax.experimental.pallas import tpu as pltpu

F32 = jnp.float32
BF16 = jnp.bfloat16

CHUNK = 64
N_HEADS = 16
N_KV_HEADS = 2
HEAD_DIM = 64
GROUP = N_HEADS // N_KV_HEADS
PAIRS = GROUP // 2
KV_DIM = N_KV_HEADS * HEAD_DIM
WINDOW = 128
CONV_WIDTH = 3
EPS = 1e-6
ATTN_SCALE = HEAD_DIM ** -0.5

KEY_SPAN = 256
LANES = 128
SUBLANES = 8
VMEM_LIMIT_BYTES = 56 * 1024 * 1024


def _rms(x, g):
    y = x * lax.rsqrt(jnp.mean(x * x, axis=-1, keepdims=True) + EPS)
    return y * g


def _dot(a, b):
    return jnp.dot(a, b, preferred_element_type=F32)


def _dot_nt(a, b):
    return lax.dot_general(a, b, (((1,), (1,)), ((), ())), preferred_element_type=F32)


def _const_spec(shape):
    nd = len(shape)
    return pl.BlockSpec(shape, lambda *_: (0,) * nd, pipeline_mode=pl.Buffered(1))


def _layer_spec(stacked, layer):
    rest = stacked.shape[1:]
    return pl.BlockSpec((None,) + rest, lambda *_: (layer,) + (0,) * len(rest),
                        pipeline_mode=pl.Buffered(1))


def _ffn_body(x_ref, g_ref, wg_ref, wu_ref, wd_ref, gf_ref, o_ref, *, final_norm, sub):
    blocks = [pl.ds(r, sub) for r in range(0, x_ref.shape[0], sub)]
    gated = []
    for rows in blocks:
        h = _rms(x_ref[rows, :], g_ref[...]).astype(BF16)
        gated.append((_dot(h, wg_ref[...]), _dot(h, wu_ref[...])))
    for rows, (gate, up) in zip(blocks, gated):
        act = (gate * jax.nn.sigmoid(gate) * up).astype(BF16)
        out = x_ref[rows, :] + 0.5 * _dot(act, wd_ref[...])
        if final_norm:
            out = _rms(out, gf_ref[...])
        o_ref[rows, :] = out


def _ffn(x2, g, wg, wu, wd, gf, *, layer, tile, sub, final_norm):
    n, d = x2.shape
    assert n % tile == 0 and tile % sub == 0
    return pl.pallas_call(
        functools.partial(_ffn_body, final_norm=final_norm, sub=sub),
        grid=(n // tile,),
        in_specs=[
            pl.BlockSpec((tile, d), lambda i: (i, 0)),
            _layer_spec(g, layer),
            _layer_spec(wg, layer),
            _layer_spec(wu, layer),
            _layer_spec(wd, layer),
            _const_spec((1, d)),
        ],
        out_specs=pl.BlockSpec((tile, d), lambda i: (i, 0)),
        out_shape=jax.ShapeDtypeStruct((n, d), F32),
        compiler_params=pltpu.CompilerParams(
            dimension_semantics=("arbitrary",), vmem_limit_bytes=VMEM_LIMIT_BYTES),
        name="ffn_final" if final_norm else "ffn",
    )(x2, g, wg, wu, wd, gf)


def _split_offsets(d):
    sizes = (N_HEADS * HEAD_DIM, KV_DIM, KV_DIM, d, d, d, d, d)
    offs = [0]
    for s in sizes:
        offs.append(offs[-1] + s)
    return offs


def _kv_placements(k2, v2):
    n = k2.shape[0]
    lo = lax.broadcasted_iota(jnp.int32, (n, LANES), 1) < HEAD_DIM
    k2r = pltpu.roll(k2, HEAD_DIM, 1)
    v2r = pltpu.roll(v2, HEAD_DIM, 1)
    ones_lo = jnp.where(lo, 1.0, 0.0)
    ones_hi = jnp.where(lo, 0.0, 1.0)

    def cat(a, b):
        return jnp.concatenate([a, b], axis=1).astype(BF16)

    ka = (jnp.where(lo, k2, 0.0).astype(BF16), jnp.where(lo, k2r, 0.0).astype(BF16))
    kb = (jnp.where(lo, 0.0, k2r).astype(BF16), jnp.where(lo, 0.0, k2).astype(BF16))
    va = (cat(jnp.where(lo, v2, 0.0), ones_lo), cat(jnp.where(lo, v2r, 0.0), ones_lo))
    vb = (cat(jnp.where(lo, 0.0, v2r), ones_hi), cat(jnp.where(lo, 0.0, v2), ones_hi))
    return ka, kb, va, vb


def _attn_probs(qs, q_row0, nq, ka, kb, sink_ref, bias):
    qrows = pl.ds(q_row0, nq)
    lo = lax.broadcasted_iota(jnp.int32, (1, LANES), 1) < HEAD_DIM
    out = []
    for h in range(N_KV_HEADS):
        base = h * GROUP * HEAD_DIM
        q_stack = jnp.concatenate(
            [qs[qrows, base + p * LANES:base + (p + 1) * LANES] for p in range(PAIRS)], axis=0)
        s0 = _dot_nt(q_stack, ka[h]) + bias
        s1 = _dot_nt(q_stack, kb[h]) + bias
        sink0 = sink_ref[h, 0]
        sink1 = sink_ref[h, 1]
        m0 = jnp.maximum(jnp.max(s0, axis=1, keepdims=True), sink0)
        m1 = jnp.maximum(jnp.max(s1, axis=1, keepdims=True), sink1)
        e0 = jnp.exp(s0 - m0).astype(BF16)
        e1 = jnp.exp(s1 - m1).astype(BF16)
        e_sink = jnp.where(lo, jnp.exp(sink0 - m0), jnp.exp(sink1 - m1))
        out.append((e0, e1, e_sink))
    return out


def _attn_output(at, q_row0, nq, probs, va, vb):
    qrows = pl.ds(q_row0, nq)
    for h in range(N_KV_HEADS):
        base = h * GROUP * HEAD_DIM
        e0, e1, e_sink = probs[h]
        o2 = _dot(e0, va[h]) + _dot(e1, vb[h])
        o = (o2[:, 0:LANES] / (o2[:, LANES:2 * LANES] + e_sink)).astype(BF16)
        for p in range(PAIRS):
            at[qrows, base + p * LANES:base + (p + 1) * LANES] = o[p * nq:(p + 1) * nq, :]


ONES_ROWS = 16


def _attn_probs_t(qs, q_row0, nq, keys, sink_ref, bias):
    qrows = pl.ds(q_row0, nq)
    out = []
    for h in range(N_KV_HEADS):
        base = h * GROUP * HEAD_DIM
        q_stack = jnp.concatenate(
            [qs[qrows, base + p * LANES:base + (p + 1) * LANES] for p in range(PAIRS)], axis=0)
        s_both = _dot_nt(keys[h], q_stack)
        n_keys = keys[h].shape[0] // 2
        per_head = []
        for i in range(2):
            s = s_both[i * n_keys:(i + 1) * n_keys, :] + bias
            sink = sink_ref[h, i]
            m = jnp.maximum(jnp.max(s, axis=0, keepdims=True), sink)
            e = jnp.exp(s - m).astype(BF16)
            if n_keys < KEY_SPAN:
                e = jnp.concatenate([jnp.zeros((KEY_SPAN - n_keys, e.shape[1]), BF16), e], axis=0)
            per_head.append((e, jnp.exp(sink - m)))
        out.append(per_head)
    return out


def _attn_output_t(at, q_row0, nq, probs, values):
    qrows = pl.ds(q_row0, nq)
    for h in range(N_KV_HEADS):
        base = h * GROUP * HEAD_DIM
        halves = []
        for e, e_sink in probs[h]:
            o_t = _dot(values[h], e)
            halves.append(o_t[0:HEAD_DIM, :] / (o_t[HEAD_DIM:HEAD_DIM + 1, :] + e_sink))
        o = jnp.concatenate(halves, axis=0).T.astype(BF16)
        for p in range(PAIRS):
            at[qrows, base + p * LANES:base + (p + 1) * LANES] = o[p * nq:(p + 1) * nq, :]


def _conv_taps(ub, cw_ref, n):
    out = ub[pl.ds(SUBLANES - 2, n), :] * cw_ref[0:1, :]
    out = out + ub[pl.ds(SUBLANES - 1, n), :] * cw_ref[1:2, :]
    return out + ub[pl.ds(SUBLANES, n), :] * cw_ref[2:3, :]


VISIBLE = WINDOW + CHUNK


def _mixer_prompt_body(x_ref, g_ref, win_ref, bin_ref, sink_ref, cw_ref, woa_ref, woc_ref, wo_ref,
                       xo_ref, kn_ref, vn_ref, un_ref,
                       qs, at, ub, *carry_refs, tile):
    d = x_ref.shape[1]
    offs = _split_offsets(d)
    t = pl.program_id(1)
    kc_refs = carry_refs[0:2 * N_KV_HEADS]
    vc_refs = carry_refs[2 * N_KV_HEADS:]
    cur = lax.rem(t, 2)
    nxt = 1 - cur

    @pl.when(t == 0)
    def _():
        for r in carry_refs:
            r[0] = jnp.zeros(r.shape[1:], r.dtype)
        ub[0, 0:SUBLANES, :] = jnp.zeros((SUBLANES, d), F32)

    x = x_ref[...]
    h = _rms(x, g_ref[...]).astype(BF16)

    def proj(i, n=1):
        return _dot(h, win_ref[:, offs[i]:offs[i + n]]) + bin_ref[:, offs[i]:offs[i + n]]

    def proj_kv():
        kv = proj(1, 2)
        return kv[:, 0:KV_DIM], kv[:, KV_DIM:2 * KV_DIM]

    k2, v2 = proj_kv()
    kn_ref[0] = k2[tile - WINDOW:, :]
    vn_ref[0] = v2[tile - WINDOW:, :]
    qs[...] = (proj(0) * ATTN_SCALE).astype(BF16)
    c_gate = proj(4)

    k_placed = [a for pair in _kv_placements(k2, v2)[0:2] for a in pair]
    k_carried = [r[cur] for r in kc_refs]

    def key_window(i, r0):
        if r0 >= WINDOW:
            return k_placed[i][r0 - WINDOW:r0 - WINDOW + VISIBLE, :]
        return jnp.concatenate([k_carried[i][r0:, :], k_placed[i][0:r0 + CHUNK, :]], axis=0)

    v_t = v2.T
    ones = jnp.ones((ONES_ROWS, tile), F32)
    width = KEY_SPAN + tile
    v_bufs = []
    for hh in range(N_KV_HEADS):
        rows = jnp.concatenate([v_t[hh * HEAD_DIM:(hh + 1) * HEAD_DIM, :], ones], axis=0)
        buf_a = jnp.concatenate([vc_refs[hh][cur], rows], axis=1)
        buf_b = pltpu.roll(buf_a, width - CHUNK, 1)
        v_bufs.append((buf_a.astype(BF16), buf_b.astype(BF16)))
        vc_refs[hh][nxt] = buf_a[:, tile:tile + KEY_SPAN]

    def value_window(hh, c):
        buf_a, buf_b = v_bufs[hh]
        if c % 2:
            return buf_a[:, (c + 1) * CHUNK:(c + 1) * CHUNK + KEY_SPAN]
        return buf_b[:, c * CHUNK:c * CHUNK + KEY_SPAN]

    j = lax.broadcasted_iota(jnp.int32, (VISIBLE, PAIRS * CHUNK), 0)
    probs = []
    for c in range(tile // CHUNK):
        r0 = c * CHUNK
        pos = t * tile + (r0 - WINDOW) + j
        bias = jnp.where(pos >= 0, 0.0, -jnp.inf).astype(F32)
        keys = [jnp.concatenate([key_window(hh, r0), key_window(N_KV_HEADS + hh, r0)], axis=0)
                for hh in range(N_KV_HEADS)]
        probs.append(_attn_probs_t(qs, r0, CHUNK, keys, sink_ref, bias))

    u = c_gate * proj(5)
    ub[cur, SUBLANES:SUBLANES + tile, :] = u
    conv = (proj(3) * _conv_taps(ub.at[cur], cw_ref, tile)).astype(BF16)
    un_ref[0] = u[tile - SUBLANES:, :]
    ub[nxt, 0:SUBLANES, :] = u[tile - SUBLANES:, :]
    gate_attn = jax.nn.sigmoid(proj(6))
    gate_conv = jax.nn.sigmoid(proj(7))
    y_conv = gate_conv * _dot(conv, woc_ref[...])

    for c in range(tile // CHUNK):
        values = [value_window(hh, c) for hh in range(N_KV_HEADS)]
        _attn_output_t(at, c * CHUNK, CHUNK, probs[c], values)

    merged = gate_attn * _dot(at[...], woa_ref[...]) + y_conv
    xo_ref[...] = x + _dot(merged.astype(BF16), wo_ref[...])

    for r, a in zip(kc_refs, k_placed):
        r[nxt] = a[tile - WINDOW:, :]


def _mixer_prompt(x2, g, w_in, b_in, sink_cols, conv_w, woa, woc, wo, *, layer, batch, tile):
    n, d = x2.shape
    seq = n // batch
    assert seq % tile == 0 and tile % (2 * CHUNK) == 0 and tile >= KEY_SPAN
    nt = seq // tile
    return pl.pallas_call(
        functools.partial(_mixer_prompt_body, tile=tile),
        grid=(batch, nt),
        in_specs=[
            pl.BlockSpec((tile, d), lambda b, t: (b * nt + t, 0)),
            _layer_spec(g, layer),
            _layer_spec(w_in, layer),
            _layer_spec(b_in, layer),
            _const_spec(sink_cols.shape),
            _layer_spec(conv_w, layer),
            _layer_spec(woa, layer),
            _layer_spec(woc, layer),
            _layer_spec(wo, layer),
        ],
        out_specs=[
            pl.BlockSpec((tile, d), lambda b, t: (b * nt + t, 0)),
            pl.BlockSpec((1, WINDOW, KV_DIM), lambda b, t: (b, 0, 0)),
            pl.BlockSpec((1, WINDOW, KV_DIM), lambda b, t: (b, 0, 0)),
            pl.BlockSpec((1, SUBLANES, d), lambda b, t: (b, 0, 0)),
        ],
        out_shape=[
            jax.ShapeDtypeStruct((n, d), F32),
            jax.ShapeDtypeStruct((batch, WINDOW, KV_DIM), F32),
            jax.ShapeDtypeStruct((batch, WINDOW, KV_DIM), F32),
            jax.ShapeDtypeStruct((batch, SUBLANES, d), F32),
        ],
        scratch_shapes=[
            pltpu.VMEM((tile, d), BF16),
            pltpu.VMEM((tile, d), BF16),
            pltpu.VMEM((2, SUBLANES + tile, d), F32),
        ] + [pltpu.VMEM((2, WINDOW, LANES), BF16) for _ in range(2 * N_KV_HEADS)]
          + [pltpu.VMEM((2, HEAD_DIM + ONES_ROWS, KEY_SPAN), F32) for _ in range(N_KV_HEADS)],
        compiler_params=pltpu.CompilerParams(
            dimension_semantics=("arbitrary", "arbitrary"), vmem_limit_bytes=VMEM_LIMIT_BYTES),
        name="mixer_prompt",
    )(x2, g, w_in, b_in, sink_cols, conv_w, woa, woc, wo)


def _mixer_sample_body(x_ref, ck_ref, cv_ref, st_ref, g_ref, win_ref, bin_ref, sink_ref, cw_ref,
                       woa_ref, woc_ref, wo_ref,
                       xo_ref, kn_ref, vn_ref, un_ref,
                       qs, at, cs, ub, *, n_streams, n_new):
    d = x_ref.shape[1]
    offs = _split_offsets(d)
    keep = ck_ref.shape[1]
    x = x_ref[...]
    h = _rms(x, g_ref[...]).astype(BF16)

    def proj(i, n=1):
        return _dot(h, win_ref[:, offs[i]:offs[i + n]]) + bin_ref[:, offs[i]:offs[i + n]]

    def proj_kv():
        kv = proj(1, 2)
        return kv[:, 0:KV_DIM], kv[:, KV_DIM:2 * KV_DIM]

    kn_all, vn_all = proj_kv()
    qs[...] = (proj(0) * ATTN_SCALE).astype(BF16)
    u_all = proj(4) * proj(5)

    pad0 = keep + n_new
    j = lax.broadcasted_iota(jnp.int32, (1, KEY_SPAN), 1)
    bias = jnp.where(j < pad0, 0.0, -jnp.inf).astype(F32)

    for b in range(n_streams):
        r0 = b * n_new
        k_new = kn_all[r0:r0 + n_new, :]
        v_new = vn_all[r0:r0 + n_new, :]
        kn_ref[b, 0:keep - n_new, :] = ck_ref[b, n_new:keep, :]
        kn_ref[b, keep - n_new:keep, :] = k_new
        vn_ref[b, 0:keep - n_new, :] = cv_ref[b, n_new:keep, :]
        vn_ref[b, keep - n_new:keep, :] = v_new
        old = _kv_placements(ck_ref[b], cv_ref[b])
        new = _kv_placements(k_new, v_new)
        ka, kb, va, vb = tuple(
            tuple(jnp.concatenate(
                [o[hh], w[hh], jnp.zeros((KEY_SPAN - pad0, o[hh].shape[1]), BF16)], axis=0)
                for hh in range(N_KV_HEADS))
            for o, w in zip(old, new))
        probs = _attn_probs(qs, r0, n_new, ka, kb, sink_ref, bias)
        _attn_output(at, r0, n_new, probs, va, vb)

        ub[b, 0:SUBLANES, :] = st_ref[b]
        ub[b, SUBLANES:SUBLANES + n_new, :] = u_all[r0:r0 + n_new, :]
        cs[r0:r0 + n_new, :] = _conv_taps(ub.at[b], cw_ref, n_new)
        un_ref[b] = u_all[r0 + n_new - SUBLANES:r0 + n_new, :]

    conv = (proj(3) * cs[...]).astype(BF16)
    merged = (jax.nn.sigmoid(proj(6)) * _dot(at[...], woa_ref[...])
              + jax.nn.sigmoid(proj(7)) * _dot(conv, woc_ref[...]))
    xo_ref[...] = x + _dot(merged.astype(BF16), wo_ref[...])


def _mixer_sample(x2, ck, cv, st, g, w_in, b_in, sink_cols, conv_w, woa, woc, wo, *, layer):
    n, d = x2.shape
    _, n_streams, keep, _ = ck.shape
    n_new = n // n_streams
    assert keep + n_new <= KEY_SPAN and n_new % SUBLANES == 0 and keep >= n_new
    args = (x2, ck, cv, st, g, w_in, b_in, sink_cols, conv_w, woa, woc, wo)

    def whole(a):
        nd = len(a.shape)
        return pl.BlockSpec(a.shape, lambda i: (0,) * nd)

    out_shape = [
        jax.ShapeDtypeStruct((n, d), F32),
        jax.ShapeDtypeStruct((n_streams, keep, KV_DIM), F32),
        jax.ShapeDtypeStruct((n_streams, keep, KV_DIM), F32),
        jax.ShapeDtypeStruct((n_streams, SUBLANES, d), F32),
    ]
    return pl.pallas_call(
        functools.partial(_mixer_sample_body, n_streams=n_streams, n_new=n_new),
        grid=(1,),
        in_specs=[whole(a) if a is x2 or a is sink_cols else _layer_spec(a, layer) for a in args],
        out_specs=[whole(s) for s in out_shape],
        out_shape=out_shape,
        scratch_shapes=[
            pltpu.VMEM((n, d), BF16),
            pltpu.VMEM((n, d), BF16),
            pltpu.VMEM((n, d), F32),
            pltpu.VMEM((n_streams, SUBLANES + n_new, d), F32),
        ],
        compiler_params=pltpu.CompilerParams(
            dimension_semantics=("arbitrary",), vmem_limit_bytes=VMEM_LIMIT_BYTES),
        name="mixer_sample",
    )(*args)


def _sink_columns(sinks_l, nq):
    s = sinks_l.astype(F32).reshape(N_KV_HEADS, PAIRS, 2)
    s = jnp.transpose(s, (0, 2, 1))
    return jnp.repeat(s, nq, axis=2)[..., None]


def _sink_rows(sinks_l, nq):
    return jnp.swapaxes(_sink_columns(sinks_l, nq), 2, 3)


def kernel(x_prompt, x_sample, cache_k, cache_v, state_conv, norm_ffn1, w1_gate, w1_up, w1_down,
           norm_mix, w_in, b_in, sinks, conv_w, w_o_attn, w_o_conv, w_o, norm_ffn2, w2_gate,
           w2_up, w2_down, norm_final):
    batch, seq, d = x_prompt.shape
    n_streams, n_new, _ = x_sample.shape
    depth = w_in.shape[0]
    keep_s = cache_k.shape[2]
    tile_ffn = 1024
    sub_ffn = 512
    tile_mix = 512

    xp = x_prompt.reshape(batch * seq, d)
    xs = x_sample.reshape(n_streams * n_new, d)
    gf = norm_final.reshape(1, d)
    st_pad = jnp.pad(state_conv, ((0, 0), (0, 0), (SUBLANES - (CONV_WIDTH - 1), 0), (0, 0)))
    ck = cache_k.reshape(depth, n_streams, keep_s, KV_DIM)
    cv = cache_v.reshape(depth, n_streams, keep_s, KV_DIM)

    w1 = (norm_ffn1.reshape(depth, 1, d), w1_gate.astype(BF16), w1_up.astype(BF16),
          w1_down.astype(BF16), gf)
    w2 = (norm_ffn2.reshape(depth, 1, d), w2_gate.astype(BF16), w2_up.astype(BF16),
          w2_down.astype(BF16), gf)
    wm = (norm_mix.reshape(depth, 1, d), w_in.astype(BF16), b_in.reshape(depth, 1, -1))
    wo = (conv_w, w_o_attn.astype(BF16), w_o_conv.astype(BF16), w_o.astype(BF16))

    kp, vp, cp, ks, vs, cs = [], [], [], [], [], []
    for l in range(depth):
        last = l == depth - 1
        xp = _ffn(xp, *w1, layer=l, tile=tile_ffn, sub=sub_ffn, final_norm=False)
        xs = _ffn(xs, *w1, layer=l, tile=xs.shape[0], sub=xs.shape[0], final_norm=False)

        xp, k_l, v_l, u_l = _mixer_prompt(
            xp, *wm, _sink_rows(sinks[l], CHUNK), *wo, layer=l, batch=batch, tile=tile_mix)
        kp.append(k_l.reshape(batch, WINDOW, N_KV_HEADS, HEAD_DIM))
        vp.append(v_l.reshape(batch, WINDOW, N_KV_HEADS, HEAD_DIM))
        cp.append(u_l[:, SUBLANES - (CONV_WIDTH - 1):, :])

        xs, k_l, v_l, u_l = _mixer_sample(
            xs, ck, cv, st_pad, *wm, _sink_columns(sinks[l], n_new), *wo, layer=l)
        ks.append(k_l.reshape(n_streams, keep_s, N_KV_HEADS, HEAD_DIM))
        vs.append(v_l.reshape(n_streams, keep_s, N_KV_HEADS, HEAD_DIM))
        cs.append(u_l[:, SUBLANES - (CONV_WIDTH - 1):, :])

        xp = _ffn(xp, *w2, layer=l, tile=tile_ffn, sub=sub_ffn, final_norm=last)
        xs = _ffn(xs, *w2, layer=l, tile=xs.shape[0], sub=xs.shape[0], final_norm=last)

    return (xp.reshape(batch, seq, d), xs.reshape(n_streams, n_new, d),
            jnp.stack(kp), jnp.stack(vp), jnp.stack(cp),
            jnp.stack(ks), jnp.stack(vs), jnp.stack(cs))
```

```python
import functools

import jax
import jax.numpy as jnp
from jax import lax
from jax.experimental import pallas as pl
from jax.experimental.pallas import tpu as pltpu

F32 = jnp.float32
BF16 = jnp.bfloat16

CHUNK = 64
N_HEADS = 16
N_KV_HEADS = 2
HEAD_DIM = 64
GROUP = N_HEADS // N_KV_HEADS
PAIRS = GROUP // 2
KV_DIM = N_KV_HEADS * HEAD_DIM
WINDOW = 128
CONV_WIDTH = 3
EPS = 1e-6
ATTN_SCALE = HEAD_DIM ** -0.5

KEY_SPAN = 256
LANES = 128
SUBLANES = 8
VMEM_LIMIT_BYTES = 56 * 1024 * 1024


def _rms(x, g):
    y = x * lax.rsqrt(jnp.mean(x * x, axis=-1, keepdims=True) + EPS)
    return y * g


def _dot(a, b):
    return jnp.dot(a, b, preferred_element_type=F32)


def _dot_nt(a, b):
    return lax.dot_general(a, b, (((1,), (1,)), ((), ())), preferred_element_type=F32)


def _const_spec(shape):
    nd = len(shape)
    return pl.BlockSpec(shape, lambda *_: (0,) * nd, pipeline_mode=pl.Buffered(1))


def _layer_spec(stacked, layer):
    rest = stacked.shape[1:]
    return pl.BlockSpec((None,) + rest, lambda *_: (layer,) + (0,) * len(rest),
                        pipeline_mode=pl.Buffered(1))


def _ffn_body(x_ref, g_ref, wg_ref, wu_ref, wd_ref, gf_ref, o_ref, *, final_norm, sub):
    blocks = [pl.ds(r, sub) for r in range(0, x_ref.shape[0], sub)]
    gated = []
    for rows in blocks:
        h = _rms(x_ref[rows, :], g_ref[...]).astype(BF16)
        gated.append((_dot(h, wg_ref[...]), _dot(h, wu_ref[...])))
    for rows, (gate, up) in zip(blocks, gated):
        act = (gate * jax.nn.sigmoid(gate) * up).astype(BF16)
        out = x_ref[rows, :] + 0.5 * _dot(act, wd_ref[...])
        if final_norm:
            out = _rms(out, gf_ref[...])
        o_ref[rows, :] = out


def _ffn(x2, g, wg, wu, wd, gf, *, layer, tile, sub, final_norm):
    n, d = x2.shape
    assert n % tile == 0 and tile % sub == 0
    return pl.pallas_call(
        functools.partial(_ffn_body, final_norm=final_norm, sub=sub),
        grid=(n // tile,),
        in_specs=[
            pl.BlockSpec((tile, d), lambda i: (i, 0)),
            _layer_spec(g, layer),
            _layer_spec(wg, layer),
            _layer_spec(wu, layer),
            _layer_spec(wd, layer),
            _const_spec((1, d)),
        ],
        out_specs=pl.BlockSpec((tile, d), lambda i: (i, 0)),
        out_shape=jax.ShapeDtypeStruct((n, d), F32),
        compiler_params=pltpu.CompilerParams(
            dimension_semantics=("arbitrary",), vmem_limit_bytes=VMEM_LIMIT_BYTES),
        name="ffn_final" if final_norm else "ffn",
    )(x2, g, wg, wu, wd, gf)


def _split_offsets(d):
    sizes = (N_HEADS * HEAD_DIM, KV_DIM, KV_DIM, d, d, d, d, d)
    offs = [0]
    for s in sizes:
        offs.append(offs[-1] + s)
    return offs


ONES_ROWS = 16


def _key_placements(k2):
    lo = lax.broadcasted_iota(jnp.int32, k2.shape, 1) < HEAD_DIM
    k2r = pltpu.roll(k2, HEAD_DIM, 1)
    return [jnp.where(lo, k2, 0.0).astype(BF16), jnp.where(lo, k2r, 0.0).astype(BF16),
            jnp.where(lo, 0.0, k2r).astype(BF16), jnp.where(lo, 0.0, k2).astype(BF16)]


def _value_rows(v2):
    v_t = v2.T
    ones = jnp.ones((ONES_ROWS, v2.shape[0]), F32)
    return [jnp.concatenate([v_t[h * HEAD_DIM:(h + 1) * HEAD_DIM, :], ones], axis=0)
            for h in range(N_KV_HEADS)]


def _attn_probs_t(qs, q_row0, nq, keys, sink_ref, bias):
    qrows = pl.ds(q_row0, nq)
    out = []
    for h in range(N_KV_HEADS):
        base = h * GROUP * HEAD_DIM
        q_stack = jnp.concatenate(
            [qs[qrows, base + p * LANES:base + (p + 1) * LANES] for p in range(PAIRS)], axis=0)
        s_both = _dot_nt(keys[h], q_stack)
        n_keys = keys[h].shape[0] // 2
        per_head = []
        for i in range(2):
            s = s_both[i * n_keys:(i + 1) * n_keys, :]
            if bias is not None:
                s = s + bias
            sink = sink_ref[h, i]
            m = jnp.maximum(jnp.max(s, axis=0, keepdims=True), sink)
            e = jnp.exp(s - m).astype(BF16)
            if n_keys < KEY_SPAN:
                e = jnp.concatenate([jnp.zeros((KEY_SPAN - n_keys, e.shape[1]), BF16), e], axis=0)
            per_head.append((e, jnp.exp(sink - m)))
        out.append(per_head)
    return out


def _attn_output_t(at, q_row0, nq, probs, values):
    qrows = pl.ds(q_row0, nq)
    for h in range(N_KV_HEADS):
        base = h * GROUP * HEAD_DIM
        halves = []
        for e, e_sink in probs[h]:
            o_t = _dot(values[h], e)
            halves.append(o_t[0:HEAD_DIM, :] / (o_t[HEAD_DIM:HEAD_DIM + 1, :] + e_sink))
        o = jnp.concatenate(halves, axis=0).T.astype(BF16)
        for p in range(PAIRS):
            at[qrows, base + p * LANES:base + (p + 1) * LANES] = o[p * nq:(p + 1) * nq, :]


def _conv_taps(ub, cw_ref, n):
    out = ub[pl.ds(SUBLANES - 2, n), :] * cw_ref[0:1, :]
    out = out + ub[pl.ds(SUBLANES - 1, n), :] * cw_ref[1:2, :]
    return out + ub[pl.ds(SUBLANES, n), :] * cw_ref[2:3, :]


VISIBLE = WINDOW + CHUNK


def _mixer_prompt_body(x_ref, g_ref, win_ref, bin_ref, sink_ref, cw_ref, woa_ref, woc_ref, wo_ref,
                       xo_ref, kn_ref, vn_ref, un_ref,
                       qs, at, ub, *carry_refs, tile):
    d = x_ref.shape[1]
    offs = _split_offsets(d)
    t = pl.program_id(1)
    kc_refs = carry_refs[0:2 * N_KV_HEADS]
    vc_refs = carry_refs[2 * N_KV_HEADS:]
    cur = lax.rem(t, 2)
    nxt = 1 - cur

    @pl.when(t == 0)
    def _():
        for r in carry_refs:
            r[0] = jnp.zeros(r.shape[1:], r.dtype)
        ub[0, 0:SUBLANES, :] = jnp.zeros((SUBLANES, d), F32)

    x = x_ref[...]
    h = _rms(x, g_ref[...]).astype(BF16)

    def proj(i, n=1):
        return _dot(h, win_ref[:, offs[i]:offs[i + n]]) + bin_ref[:, offs[i]:offs[i + n]]

    def proj_kv():
        kv = proj(1, 2)
        return kv[:, 0:KV_DIM], kv[:, KV_DIM:2 * KV_DIM]

    k2, v2 = proj_kv()
    kn_ref[0] = k2[tile - WINDOW:, :]
    vn_ref[0] = v2[tile - WINDOW:, :]
    qs[...] = (proj(0) * ATTN_SCALE).astype(BF16)
    c_gate = proj(4)

    k_placed = _key_placements(k2)
    k_carried = [r[cur] for r in kc_refs]

    def key_window(i, r0):
        if r0 >= WINDOW:
            return k_placed[i][r0 - WINDOW:r0 - WINDOW + VISIBLE, :]
        return jnp.concatenate([k_carried[i][r0:, :], k_placed[i][0:r0 + CHUNK, :]], axis=0)

    width = KEY_SPAN + tile
    v_bufs = []
    for hh, rows in enumerate(_value_rows(v2)):
        buf_a = jnp.concatenate([vc_refs[hh][cur], rows], axis=1)
        buf_b = pltpu.roll(buf_a, width - CHUNK, 1)
        v_bufs.append((buf_a.astype(BF16), buf_b.astype(BF16)))
        vc_refs[hh][nxt] = buf_a[:, tile:tile + KEY_SPAN]

    def value_window(hh, c):
        buf_a, buf_b = v_bufs[hh]
        if c % 2:
            return buf_a[:, (c + 1) * CHUNK:(c + 1) * CHUNK + KEY_SPAN]
        return buf_b[:, c * CHUNK:c * CHUNK + KEY_SPAN]

    j = lax.broadcasted_iota(jnp.int32, (VISIBLE, PAIRS * CHUNK), 0)
    probs = []
    for c in range(tile // CHUNK):
        r0 = c * CHUNK
        pos = t * tile + (r0 - WINDOW) + j
        bias = jnp.where(pos >= 0, 0.0, -jnp.inf).astype(F32)
        keys = [jnp.concatenate([key_window(hh, r0), key_window(N_KV_HEADS + hh, r0)], axis=0)
                for hh in range(N_KV_HEADS)]
        probs.append(_attn_probs_t(qs, r0, CHUNK, keys, sink_ref, bias))

    u = c_gate * proj(5)
    ub[cur, SUBLANES:SUBLANES + tile, :] = u
    conv = (proj(3) * _conv_taps(ub.at[cur], cw_ref, tile)).astype(BF16)
    un_ref[0] = u[tile - SUBLANES:, :]
    ub[nxt, 0:SUBLANES, :] = u[tile - SUBLANES:, :]
    gate_attn = jax.nn.sigmoid(proj(6))
    gate_conv = jax.nn.sigmoid(proj(7))
    y_conv = gate_conv * _dot(conv, woc_ref[...])

    for c in range(tile // CHUNK):
        values = [value_window(hh, c) for hh in range(N_KV_HEADS)]
        _attn_output_t(at, c * CHUNK, CHUNK, probs[c], values)

    merged = gate_attn * _dot(at[...], woa_ref[...]) + y_conv
    xo_ref[...] = x + _dot(merged.astype(BF16), wo_ref[...])

    for r, a in zip(kc_refs, k_placed):
        r[nxt] = a[tile - WINDOW:, :]


def _mixer_prompt(x2, g, w_in, b_in, sink_cols, conv_w, woa, woc, wo, *, layer, batch, tile):
    n, d = x2.shape
    seq = n // batch
    assert seq % tile == 0 and tile % (2 * CHUNK) == 0 and tile >= KEY_SPAN
    nt = seq // tile
    return pl.pallas_call(
        functools.partial(_mixer_prompt_body, tile=tile),
        grid=(batch, nt),
        in_specs=[
            pl.BlockSpec((tile, d), lambda b, t: (b * nt + t, 0)),
            _layer_spec(g, layer),
            _layer_spec(w_in, layer),
            _layer_spec(b_in, layer),
            _const_spec(sink_cols.shape),
            _layer_spec(conv_w, layer),
            _layer_spec(woa, layer),
            _layer_spec(woc, layer),
            _layer_spec(wo, layer),
        ],
        out_specs=[
            pl.BlockSpec((tile, d), lambda b, t: (b * nt + t, 0)),
            pl.BlockSpec((1, WINDOW, KV_DIM), lambda b, t: (b, 0, 0)),
            pl.BlockSpec((1, WINDOW, KV_DIM), lambda b, t: (b, 0, 0)),
            pl.BlockSpec((1, SUBLANES, d), lambda b, t: (b, 0, 0)),
        ],
        out_shape=[
            jax.ShapeDtypeStruct((n, d), F32),
            jax.ShapeDtypeStruct((batch, WINDOW, KV_DIM), F32),
            jax.ShapeDtypeStruct((batch, WINDOW, KV_DIM), F32),
            jax.ShapeDtypeStruct((batch, SUBLANES, d), F32),
        ],
        scratch_shapes=[
            pltpu.VMEM((tile, d), BF16),
            pltpu.VMEM((tile, d), BF16),
            pltpu.VMEM((2, SUBLANES + tile, d), F32),
        ] + [pltpu.VMEM((2, WINDOW, LANES), BF16) for _ in range(2 * N_KV_HEADS)]
          + [pltpu.VMEM((2, HEAD_DIM + ONES_ROWS, KEY_SPAN), F32) for _ in range(N_KV_HEADS)],
        compiler_params=pltpu.CompilerParams(
            dimension_semantics=("arbitrary", "arbitrary"), vmem_limit_bytes=VMEM_LIMIT_BYTES),
        name="mixer_prompt",
    )(x2, g, w_in, b_in, sink_cols, conv_w, woa, woc, wo)


def _mixer_sample_body(x_ref, ck_ref, cv_ref, st_ref, g_ref, win_ref, bin_ref, sink_ref, cw_ref,
                       woa_ref, woc_ref, wo_ref,
                       xo_ref, kn_ref, vn_ref, un_ref,
                       qs, at, cs, ub, *, n_streams, n_new):
    d = x_ref.shape[1]
    offs = _split_offsets(d)
    keep = ck_ref.shape[1]
    x = x_ref[...]
    h = _rms(x, g_ref[...]).astype(BF16)

    def proj(i, n=1):
        return _dot(h, win_ref[:, offs[i]:offs[i + n]]) + bin_ref[:, offs[i]:offs[i + n]]

    def proj_kv():
        kv = proj(1, 2)
        return kv[:, 0:KV_DIM], kv[:, KV_DIM:2 * KV_DIM]

    kn_all, vn_all = proj_kv()
    qs[...] = (proj(0) * ATTN_SCALE).astype(BF16)
    u_all = proj(4) * proj(5)

    pad = jnp.zeros((KEY_SPAN - keep - n_new, KV_DIM), F32)

    for b in range(n_streams):
        r0 = b * n_new
        k_new = kn_all[r0:r0 + n_new, :]
        v_new = vn_all[r0:r0 + n_new, :]
        kn_ref[b, 0:keep - n_new, :] = ck_ref[b, n_new:keep, :]
        kn_ref[b, keep - n_new:keep, :] = k_new
        vn_ref[b, 0:keep - n_new, :] = cv_ref[b, n_new:keep, :]
        vn_ref[b, keep - n_new:keep, :] = v_new
        placed = _key_placements(jnp.concatenate([ck_ref[b], k_new], axis=0))
        keys = [jnp.concatenate([placed[hh], placed[N_KV_HEADS + hh]], axis=0)
                for hh in range(N_KV_HEADS)]
        values = [rows.astype(BF16)
                  for rows in _value_rows(jnp.concatenate([pad, cv_ref[b], v_new], axis=0))]
        probs = _attn_probs_t(qs, r0, n_new, keys, sink_ref, None)
        _attn_output_t(at, r0, n_new, probs, values)

        ub[b, 0:SUBLANES, :] = st_ref[b]
        ub[b, SUBLANES:SUBLANES + n_new, :] = u_all[r0:r0 + n_new, :]
        cs[r0:r0 + n_new, :] = _conv_taps(ub.at[b], cw_ref, n_new)
        un_ref[b] = u_all[r0 + n_new - SUBLANES:r0 + n_new, :]

    conv = (proj(3) * cs[...]).astype(BF16)
    merged = (jax.nn.sigmoid(proj(6)) * _dot(at[...], woa_ref[...])
              + jax.nn.sigmoid(proj(7)) * _dot(conv, woc_ref[...]))
    xo_ref[...] = x + _dot(merged.astype(BF16), wo_ref[...])


def _mixer_sample(x2, ck, cv, st, g, w_in, b_in, sink_cols, conv_w, woa, woc, wo, *, layer):
    n, d = x2.shape
    _, n_streams, keep, _ = ck.shape
    n_new = n // n_streams
    assert keep + n_new <= KEY_SPAN and n_new % SUBLANES == 0 and keep >= n_new
    args = (x2, ck, cv, st, g, w_in, b_in, sink_cols, conv_w, woa, woc, wo)

    def whole(a):
        nd = len(a.shape)
        return pl.BlockSpec(a.shape, lambda i: (0,) * nd)

    out_shape = [
        jax.ShapeDtypeStruct((n, d), F32),
        jax.ShapeDtypeStruct((n_streams, keep, KV_DIM), F32),
        jax.ShapeDtypeStruct((n_streams, keep, KV_DIM), F32),
        jax.ShapeDtypeStruct((n_streams, SUBLANES, d), F32),
    ]
    return pl.pallas_call(
        functools.partial(_mixer_sample_body, n_streams=n_streams, n_new=n_new),
        grid=(1,),
        in_specs=[whole(a) if a is x2 or a is sink_cols else _layer_spec(a, layer) for a in args],
        out_specs=[whole(s) for s in out_shape],
        out_shape=out_shape,
        scratch_shapes=[
            pltpu.VMEM((n, d), BF16),
            pltpu.VMEM((n, d), BF16),
            pltpu.VMEM((n, d), F32),
            pltpu.VMEM((n_streams, SUBLANES + n_new, d), F32),
        ],
        compiler_params=pltpu.CompilerParams(
            dimension_semantics=("arbitrary",), vmem_limit_bytes=VMEM_LIMIT_BYTES),
        name="mixer_sample",
    )(*args)


FFN_SUB_ROWS = 512
FFN_TILE_ROWS = 1024
MIXER_TILE_ROWS = 512


def _tile_sizes(n_rows, seq):
    sub = FFN_SUB_ROWS if n_rows % FFN_SUB_ROWS == 0 else n_rows
    tile = FFN_TILE_ROWS if n_rows % FFN_TILE_ROWS == 0 else sub
    mix = MIXER_TILE_ROWS if seq % MIXER_TILE_ROWS == 0 else seq
    return tile, sub, mix


def _sink_rows(sinks_l, nq):
    s = sinks_l.astype(F32).reshape(N_KV_HEADS, PAIRS, 2)
    s = jnp.transpose(s, (0, 2, 1))
    return jnp.repeat(s, nq, axis=2)[:, :, None, :]


def kernel(x_prompt, x_sample, cache_k, cache_v, state_conv, norm_ffn1, w1_gate, w1_up, w1_down,
           norm_mix, w_in, b_in, sinks, conv_w, w_o_attn, w_o_conv, w_o, norm_ffn2, w2_gate,
           w2_up, w2_down, norm_final):
    batch, seq, d = x_prompt.shape
    n_streams, n_new, _ = x_sample.shape
    depth = w_in.shape[0]
    keep_s = cache_k.shape[2]
    tile_ffn, sub_ffn, tile_mix = _tile_sizes(batch * seq, seq)

    xp = x_prompt.reshape(batch * seq, d)
    xs = x_sample.reshape(n_streams * n_new, d)
    gf = norm_final.reshape(1, d)
    st_pad = jnp.pad(state_conv, ((0, 0), (0, 0), (SUBLANES - (CONV_WIDTH - 1), 0), (0, 0)))
    ck = cache_k.reshape(depth, n_streams, keep_s, KV_DIM)
    cv = cache_v.reshape(depth, n_streams, keep_s, KV_DIM)

    w1 = (norm_ffn1.reshape(depth, 1, d), w1_gate.astype(BF16), w1_up.astype(BF16),
          w1_down.astype(BF16), gf)
    w2 = (norm_ffn2.reshape(depth, 1, d), w2_gate.astype(BF16), w2_up.astype(BF16),
          w2_down.astype(BF16), gf)
    wm = (norm_mix.reshape(depth, 1, d), w_in.astype(BF16), b_in.reshape(depth, 1, -1))
    wo = (conv_w, w_o_attn.astype(BF16), w_o_conv.astype(BF16), w_o.astype(BF16))

    kp, vp, cp, ks, vs, cs = [], [], [], [], [], []
    for l in range(depth):
        last = l == depth - 1
        xp = _ffn(xp, *w1, layer=l, tile=tile_ffn, sub=sub_ffn, final_norm=False)
        xs = _ffn(xs, *w1, layer=l, tile=xs.shape[0], sub=xs.shape[0], final_norm=False)

        xp, k_l, v_l, u_l = _mixer_prompt(
            xp, *wm, _sink_rows(sinks[l], CHUNK), *wo, layer=l, batch=batch, tile=tile_mix)
        kp.append(k_l.reshape(batch, WINDOW, N_KV_HEADS, HEAD_DIM))
        vp.append(v_l.reshape(batch, WINDOW, N_KV_HEADS, HEAD_DIM))
        cp.append(u_l[:, SUBLANES - (CONV_WIDTH - 1):, :])

        xs, k_l, v_l, u_l = _mixer_sample(
            xs, ck, cv, st_pad, *wm, _sink_rows(sinks[l], n_new), *wo, layer=l)
        ks.append(k_l.reshape(n_streams, keep_s, N_KV_HEADS, HEAD_DIM))
        vs.append(v_l.reshape(n_streams, keep_s, N_KV_HEADS, HEAD_DIM))
        cs.append(u_l[:, SUBLANES - (CONV_WIDTH - 1):, :])

        xp = _ffn(xp, *w2, layer=l, tile=tile_ffn, sub=sub_ffn, final_norm=last)
        xs = _ffn(xs, *w2, layer=l, tile=xs.shape[0], sub=xs.shape[0], final_norm=last)

    return (xp.reshape(batch, seq, d), xs.reshape(n_streams, n_new, d),
            jnp.stack(kp), jnp.stack(vp), jnp.stack(cp),
            jnp.stack(ks), jnp.stack(vs), jnp.stack(cs))
```

```python
import functools

import jax
import jax.numpy as jnp
from jax import lax
from jax.experimental import pallas as pl
from jax.experimental.pallas import tpu as pltpu

F32 = jnp.float32
BF16 = jnp.bfloat16

CHUNK = 64
N_HEADS = 16
N_KV_HEADS = 2
HEAD_DIM = 64
GROUP = N_HEADS // N_KV_HEADS
PAIRS = GROUP // 2
KV_DIM = N_KV_HEADS * HEAD_DIM
WINDOW = 128
CONV_WIDTH = 3
EPS = 1e-6
ATTN_SCALE = HEAD_DIM ** -0.5

KEY_SPAN = 256
LANES = 128
SUBLANES = 8
VMEM_LIMIT_BYTES = 56 * 1024 * 1024


def _rms(x, g):
    y = x * lax.rsqrt(jnp.mean(x * x, axis=-1, keepdims=True) + EPS)
    return y * g


def _dot(a, b):
    return jnp.dot(a, b, preferred_element_type=F32)


def _dot_nt(a, b):
    return lax.dot_general(a, b, (((1,), (1,)), ((), ())), preferred_element_type=F32)


def _const_spec(shape):
    nd = len(shape)
    return pl.BlockSpec(shape, lambda *_: (0,) * nd, pipeline_mode=pl.Buffered(1))


def _layer_spec(stacked, layer):
    rest = stacked.shape[1:]
    return pl.BlockSpec((None,) + rest, lambda *_: (layer,) + (0,) * len(rest),
                        pipeline_mode=pl.Buffered(1))


def _ffn_body(x_ref, g_ref, wg_ref, wu_ref, wd_ref, gf_ref, o_ref, *, final_norm, sub):
    blocks = [pl.ds(r, sub) for r in range(0, x_ref.shape[0], sub)]
    gated = []
    for rows in blocks:
        h = _rms(x_ref[rows, :], g_ref[...]).astype(BF16)
        gated.append((_dot(h, wg_ref[...]), _dot(h, wu_ref[...])))
    for rows, (gate, up) in zip(blocks, gated):
        act = (gate * jax.nn.sigmoid(gate) * up).astype(BF16)
        out = x_ref[rows, :] + 0.5 * _dot(act, wd_ref[...])
        if final_norm:
            out = _rms(out, gf_ref[...])
        o_ref[rows, :] = out


def _ffn(x2, g, wg, wu, wd, gf, *, layer, tile, sub, final_norm):
    n, d = x2.shape
    assert n % tile == 0 and tile % sub == 0
    return pl.pallas_call(
        functools.partial(_ffn_body, final_norm=final_norm, sub=sub),
        grid=(n // tile,),
        in_specs=[
            pl.BlockSpec((tile, d), lambda i: (i, 0)),
            _layer_spec(g, layer),
            _layer_spec(wg, layer),
            _layer_spec(wu, layer),
            _layer_spec(wd, layer),
            _const_spec((1, d)),
        ],
        out_specs=pl.BlockSpec((tile, d), lambda i: (i, 0)),
        out_shape=jax.ShapeDtypeStruct((n, d), F32),
        compiler_params=pltpu.CompilerParams(
            dimension_semantics=("arbitrary",), vmem_limit_bytes=VMEM_LIMIT_BYTES),
        name="ffn_final" if final_norm else "ffn",
    )(x2, g, wg, wu, wd, gf)


def _split_offsets(d):
    sizes = (N_HEADS * HEAD_DIM, KV_DIM, KV_DIM, d, d, d, d, d)
    offs = [0]
    for s in sizes:
        offs.append(offs[-1] + s)
    return offs


ONES_ROWS = 16


def _key_placements(k2):
    lo = lax.broadcasted_iota(jnp.int32, k2.shape, 1) < HEAD_DIM
    k2r = pltpu.roll(k2, HEAD_DIM, 1)
    return [jnp.where(lo, k2, 0.0).astype(BF16), jnp.where(lo, k2r, 0.0).astype(BF16),
            jnp.where(lo, 0.0, k2r).astype(BF16), jnp.where(lo, 0.0, k2).astype(BF16)]


def _value_rows(v2):
    v_t = v2.T
    ones = jnp.ones((ONES_ROWS, v2.shape[0]), F32)
    return [jnp.concatenate([v_t[h * HEAD_DIM:(h + 1) * HEAD_DIM, :], ones], axis=0)
            for h in range(N_KV_HEADS)]


def _attn_probs_t(qs, q_row0, nq, keys, sink_ref, bias, n_window):
    qrows = pl.ds(q_row0, nq)
    out = []
    for h in range(N_KV_HEADS):
        base = h * GROUP * HEAD_DIM
        q_stack = jnp.concatenate(
            [qs[qrows, base + p * LANES:base + (p + 1) * LANES] for p in range(PAIRS)], axis=0)
        s_both = _dot_nt(keys[h], q_stack)
        n_keys = keys[h].shape[0] // 2
        per_head = []
        for i in range(2):
            s = s_both[i * n_keys:(i + 1) * n_keys, :]
            if bias is not None:
                s = s + bias
            sink = sink_ref[h, i]
            m = jnp.maximum(jnp.max(s, axis=0, keepdims=True), sink)
            e = jnp.exp(s - m).astype(BF16)
            if n_keys < n_window:
                e = jnp.concatenate([jnp.zeros((n_window - n_keys, e.shape[1]), BF16), e], axis=0)
            per_head.append((e, jnp.exp(sink - m)))
        out.append(per_head)
    return out


def _attn_output_t(at, q_row0, nq, probs, values):
    qrows = pl.ds(q_row0, nq)
    for h in range(N_KV_HEADS):
        base = h * GROUP * HEAD_DIM
        halves = []
        for e, e_sink in probs[h]:
            o_t = _dot(values[h], e)
            halves.append(o_t[0:HEAD_DIM, :] / (o_t[HEAD_DIM:HEAD_DIM + 1, :] + e_sink))
        o = jnp.concatenate(halves, axis=0).T.astype(BF16)
        for p in range(PAIRS):
            at[qrows, base + p * LANES:base + (p + 1) * LANES] = o[p * nq:(p + 1) * nq, :]


def _conv_taps(ub, cw_ref, n):
    out = ub[pl.ds(SUBLANES - 2, n), :] * cw_ref[0:1, :]
    out = out + ub[pl.ds(SUBLANES - 1, n), :] * cw_ref[1:2, :]
    return out + ub[pl.ds(SUBLANES, n), :] * cw_ref[2:3, :]


VISIBLE = WINDOW + CHUNK


def _mixer_prompt_body(x_ref, g_ref, win_ref, bin_ref, sink_ref, cw_ref, woa_ref, woc_ref, wo_ref,
                       xo_ref, kn_ref, vn_ref, un_ref,
                       qs, at, ub, *carry_refs, tile):
    d = x_ref.shape[1]
    offs = _split_offsets(d)
    t = pl.program_id(1)
    kc_refs = carry_refs[0:2 * N_KV_HEADS]
    vc_refs = carry_refs[2 * N_KV_HEADS:]
    cur = lax.rem(t, 2)
    nxt = 1 - cur

    @pl.when(t == 0)
    def _():
        for r in carry_refs:
            r[0] = jnp.zeros(r.shape[1:], r.dtype)
        ub[0, 0:SUBLANES, :] = jnp.zeros((SUBLANES, d), F32)

    x = x_ref[...]
    h = _rms(x, g_ref[...]).astype(BF16)

    def proj(i, n=1):
        return _dot(h, win_ref[:, offs[i]:offs[i + n]]) + bin_ref[:, offs[i]:offs[i + n]]

    def proj_kv():
        kv = proj(1, 2)
        return kv[:, 0:KV_DIM], kv[:, KV_DIM:2 * KV_DIM]

    k2, v2 = proj_kv()
    kn_ref[0] = k2[tile - WINDOW:, :]
    vn_ref[0] = v2[tile - WINDOW:, :]
    qs[...] = (proj(0) * ATTN_SCALE).astype(BF16)
    c_gate = proj(4)

    k_placed = _key_placements(k2)
    k_carried = [r[cur] for r in kc_refs]

    def key_window(i, r0):
        if r0 >= WINDOW:
            return k_placed[i][r0 - WINDOW:r0 - WINDOW + VISIBLE, :]
        return jnp.concatenate([k_carried[i][r0:, :], k_placed[i][0:r0 + CHUNK, :]], axis=0)

    width = KEY_SPAN + tile
    v_bufs = []
    for hh, rows in enumerate(_value_rows(v2)):
        buf_a = jnp.concatenate([vc_refs[hh][cur], rows], axis=1)
        buf_b = pltpu.roll(buf_a, width - CHUNK, 1)
        v_bufs.append((buf_a.astype(BF16), buf_b.astype(BF16)))
        vc_refs[hh][nxt] = buf_a[:, tile:tile + KEY_SPAN]

    def value_window(hh, c):
        buf_a, buf_b = v_bufs[hh]
        if c % 2:
            return buf_b[:, (c + 1) * CHUNK:(c + 1) * CHUNK + VISIBLE]
        return buf_a[:, (c + 2) * CHUNK:(c + 2) * CHUNK + VISIBLE]

    j = lax.broadcasted_iota(jnp.int32, (VISIBLE, PAIRS * CHUNK), 0)
    probs = []
    for c in range(tile // CHUNK):
        r0 = c * CHUNK
        pos = t * tile + (r0 - WINDOW) + j
        bias = jnp.where(pos >= 0, 0.0, -jnp.inf).astype(F32)
        keys = [jnp.concatenate([key_window(hh, r0), key_window(N_KV_HEADS + hh, r0)], axis=0)
                for hh in range(N_KV_HEADS)]
        probs.append(_attn_probs_t(qs, r0, CHUNK, keys, sink_ref, bias, VISIBLE))

    u = c_gate * proj(5)
    ub[cur, SUBLANES:SUBLANES + tile, :] = u
    conv = (proj(3) * _conv_taps(ub.at[cur], cw_ref, tile)).astype(BF16)
    un_ref[0] = u[tile - SUBLANES:, :]
    ub[nxt, 0:SUBLANES, :] = u[tile - SUBLANES:, :]
    gate_attn = jax.nn.sigmoid(proj(6))
    gate_conv = jax.nn.sigmoid(proj(7))

    for c in range(tile // CHUNK):
        values = [value_window(hh, c) for hh in range(N_KV_HEADS)]
        _attn_output_t(at, c * CHUNK, CHUNK, probs[c], values)

    y_conv = gate_conv * _dot(conv, woc_ref[...])
    merged = gate_attn * _dot(at[...], woa_ref[...]) + y_conv
    xo_ref[...] = x + _dot(merged.astype(BF16), wo_ref[...])

    for r, a in zip(kc_refs, k_placed):
        r[nxt] = a[tile - WINDOW:, :]


def _mixer_prompt(x2, g, w_in, b_in, sink_cols, conv_w, woa, woc, wo, *, layer, batch, tile):
    n, d = x2.shape
    seq = n // batch
    assert seq % tile == 0 and tile % (2 * CHUNK) == 0 and tile >= KEY_SPAN
    nt = seq // tile
    return pl.pallas_call(
        functools.partial(_mixer_prompt_body, tile=tile),
        grid=(batch, nt),
        in_specs=[
            pl.BlockSpec((tile, d), lambda b, t: (b * nt + t, 0)),
            _layer_spec(g, layer),
            _layer_spec(w_in, layer),
            _layer_spec(b_in, layer),
            _const_spec(sink_cols.shape),
            _layer_spec(conv_w, layer),
            _layer_spec(woa, layer),
            _layer_spec(woc, layer),
            _layer_spec(wo, layer),
        ],
        out_specs=[
            pl.BlockSpec((tile, d), lambda b, t: (b * nt + t, 0)),
            pl.BlockSpec((1, WINDOW, KV_DIM), lambda b, t: (b, 0, 0)),
            pl.BlockSpec((1, WINDOW, KV_DIM), lambda b, t: (b, 0, 0)),
            pl.BlockSpec((1, SUBLANES, d), lambda b, t: (b, 0, 0)),
        ],
        out_shape=[
            jax.ShapeDtypeStruct((n, d), F32),
            jax.ShapeDtypeStruct((batch, WINDOW, KV_DIM), F32),
            jax.ShapeDtypeStruct((batch, WINDOW, KV_DIM), F32),
            jax.ShapeDtypeStruct((batch, SUBLANES, d), F32),
        ],
        scratch_shapes=[
            pltpu.VMEM((tile, d), BF16),
            pltpu.VMEM((tile, d), BF16),
            pltpu.VMEM((2, SUBLANES + tile, d), F32),
        ] + [pltpu.VMEM((2, WINDOW, LANES), BF16) for _ in range(2 * N_KV_HEADS)]
          + [pltpu.VMEM((2, HEAD_DIM + ONES_ROWS, KEY_SPAN), F32) for _ in range(N_KV_HEADS)],
        compiler_params=pltpu.CompilerParams(
            dimension_semantics=("arbitrary", "arbitrary"), vmem_limit_bytes=VMEM_LIMIT_BYTES),
        name="mixer_prompt",
    )(x2, g, w_in, b_in, sink_cols, conv_w, woa, woc, wo)


def _mixer_sample_body(x_ref, ck_ref, cv_ref, st_ref, g_ref, win_ref, bin_ref, sink_ref, cw_ref,
                       woa_ref, woc_ref, wo_ref,
                       xo_ref, kn_ref, vn_ref, un_ref,
                       qs, at, cs, ub, *, n_streams, n_new):
    d = x_ref.shape[1]
    offs = _split_offsets(d)
    keep = ck_ref.shape[1]
    x = x_ref[...]
    h = _rms(x, g_ref[...]).astype(BF16)

    def proj(i, n=1):
        return _dot(h, win_ref[:, offs[i]:offs[i + n]]) + bin_ref[:, offs[i]:offs[i + n]]

    def proj_kv():
        kv = proj(1, 2)
        return kv[:, 0:KV_DIM], kv[:, KV_DIM:2 * KV_DIM]

    kn_all, vn_all = proj_kv()
    qs[...] = (proj(0) * ATTN_SCALE).astype(BF16)
    u_all = proj(4) * proj(5)

    pad = jnp.zeros((KEY_SPAN - keep - n_new, KV_DIM), F32)

    for b in range(n_streams):
        r0 = b * n_new
        k_new = kn_all[r0:r0 + n_new, :]
        v_new = vn_all[r0:r0 + n_new, :]
        kn_ref[b, 0:keep - n_new, :] = ck_ref[b, n_new:keep, :]
        kn_ref[b, keep - n_new:keep, :] = k_new
        vn_ref[b, 0:keep - n_new, :] = cv_ref[b, n_new:keep, :]
        vn_ref[b, keep - n_new:keep, :] = v_new
        placed = _key_placements(jnp.concatenate([ck_ref[b], k_new], axis=0))
        keys = [jnp.concatenate([placed[hh], placed[N_KV_HEADS + hh]], axis=0)
                for hh in range(N_KV_HEADS)]
        values = [rows.astype(BF16)
                  for rows in _value_rows(jnp.concatenate([pad, cv_ref[b], v_new], axis=0))]
        probs = _attn_probs_t(qs, r0, n_new, keys, sink_ref, None, KEY_SPAN)
        _attn_output_t(at, r0, n_new, probs, values)

        ub[b, 0:SUBLANES, :] = st_ref[b]
        ub[b, SUBLANES:SUBLANES + n_new, :] = u_all[r0:r0 + n_new, :]
        cs[r0:r0 + n_new, :] = _conv_taps(ub.at[b], cw_ref, n_new)
        un_ref[b] = u_all[r0 + n_new - SUBLANES:r0 + n_new, :]

    conv = (proj(3) * cs[...]).astype(BF16)
    merged = (jax.nn.sigmoid(proj(6)) * _dot(at[...], woa_ref[...])
              + jax.nn.sigmoid(proj(7)) * _dot(conv, woc_ref[...]))
    xo_ref[...] = x + _dot(merged.astype(BF16), wo_ref[...])


def _mixer_sample(x2, ck, cv, st, g, w_in, b_in, sink_cols, conv_w, woa, woc, wo, *, layer):
    n, d = x2.shape
    _, n_streams, keep, _ = ck.shape
    n_new = n // n_streams
    assert keep + n_new <= KEY_SPAN and n_new % SUBLANES == 0 and keep >= n_new
    args = (x2, ck, cv, st, g, w_in, b_in, sink_cols, conv_w, woa, woc, wo)

    def whole(a):
        nd = len(a.shape)
        return pl.BlockSpec(a.shape, lambda i: (0,) * nd)

    out_shape = [
        jax.ShapeDtypeStruct((n, d), F32),
        jax.ShapeDtypeStruct((n_streams, keep, KV_DIM), F32),
        jax.ShapeDtypeStruct((n_streams, keep, KV_DIM), F32),
        jax.ShapeDtypeStruct((n_streams, SUBLANES, d), F32),
    ]
    return pl.pallas_call(
        functools.partial(_mixer_sample_body, n_streams=n_streams, n_new=n_new),
        grid=(1,),
        in_specs=[whole(a) if a is x2 or a is sink_cols else _layer_spec(a, layer) for a in args],
        out_specs=[whole(s) for s in out_shape],
        out_shape=out_shape,
        scratch_shapes=[
            pltpu.VMEM((n, d), BF16),
            pltpu.VMEM((n, d), BF16),
            pltpu.VMEM((n, d), F32),
            pltpu.VMEM((n_streams, SUBLANES + n_new, d), F32),
        ],
        compiler_params=pltpu.CompilerParams(
            dimension_semantics=("arbitrary",), vmem_limit_bytes=VMEM_LIMIT_BYTES),
        name="mixer_sample",
    )(*args)


FFN_SUB_ROWS = 512
FFN_TILE_ROWS = 1024
MIXER_TILE_ROWS = 512


def _tile_sizes(n_rows, seq):
    sub = FFN_SUB_ROWS if n_rows % FFN_SUB_ROWS == 0 else n_rows
    tile = FFN_TILE_ROWS if n_rows % FFN_TILE_ROWS == 0 else sub
    mix = MIXER_TILE_ROWS if seq % MIXER_TILE_ROWS == 0 else seq
    return tile, sub, mix


def _sink_rows(sinks_l, nq):
    s = sinks_l.astype(F32).reshape(N_KV_HEADS, PAIRS, 2)
    s = jnp.transpose(s, (0, 2, 1))
    return jnp.repeat(s, nq, axis=2)[:, :, None, :]


def kernel(x_prompt, x_sample, cache_k, cache_v, state_conv, norm_ffn1, w1_gate, w1_up, w1_down,
           norm_mix, w_in, b_in, sinks, conv_w, w_o_attn, w_o_conv, w_o, norm_ffn2, w2_gate,
           w2_up, w2_down, norm_final):
    batch, seq, d = x_prompt.shape
    n_streams, n_new, _ = x_sample.shape
    depth = w_in.shape[0]
    keep_s = cache_k.shape[2]
    tile_ffn, sub_ffn, tile_mix = _tile_sizes(batch * seq, seq)

    xp = x_prompt.reshape(batch * seq, d)
    xs = x_sample.reshape(n_streams * n_new, d)
    gf = norm_final.reshape(1, d)
    st_pad = jnp.pad(state_conv, ((0, 0), (0, 0), (SUBLANES - (CONV_WIDTH - 1), 0), (0, 0)))
    ck = cache_k.reshape(depth, n_streams, keep_s, KV_DIM)
    cv = cache_v.reshape(depth, n_streams, keep_s, KV_DIM)

    w1 = (norm_ffn1.reshape(depth, 1, d), w1_gate.astype(BF16), w1_up.astype(BF16),
          w1_down.astype(BF16), gf)
    w2 = (norm_ffn2.reshape(depth, 1, d), w2_gate.astype(BF16), w2_up.astype(BF16),
          w2_down.astype(BF16), gf)
    wm = (norm_mix.reshape(depth, 1, d), w_in.astype(BF16), b_in.reshape(depth, 1, -1))
    wo = (conv_w, w_o_attn.astype(BF16), w_o_conv.astype(BF16), w_o.astype(BF16))

    kp, vp, cp, ks, vs, cs = [], [], [], [], [], []
    for l in range(depth):
        last = l == depth - 1
        xp = _ffn(xp, *w1, layer=l, tile=tile_ffn, sub=sub_ffn, final_norm=False)
        xs = _ffn(xs, *w1, layer=l, tile=xs.shape[0], sub=xs.shape[0], final_norm=False)

        xp, k_l, v_l, u_l = _mixer_prompt(
            xp, *wm, _sink_rows(sinks[l], CHUNK), *wo, layer=l, batch=batch, tile=tile_mix)
        kp.append(k_l.reshape(batch, WINDOW, N_KV_HEADS, HEAD_DIM))
        vp.append(v_l.reshape(batch, WINDOW, N_KV_HEADS, HEAD_DIM))
        cp.append(u_l[:, SUBLANES - (CONV_WIDTH - 1):, :])

        xs, k_l, v_l, u_l = _mixer_sample(
            xs, ck, cv, st_pad, *wm, _sink_rows(sinks[l], n_new), *wo, layer=l)
        ks.append(k_l.reshape(n_streams, keep_s, N_KV_HEADS, HEAD_DIM))
        vs.append(v_l.reshape(n_streams, keep_s, N_KV_HEADS, HEAD_DIM))
        cs.append(u_l[:, SUBLANES - (CONV_WIDTH - 1):, :])

        xp = _ffn(xp, *w2, layer=l, tile=tile_ffn, sub=sub_ffn, final_norm=last)
        xs = _ffn(xs, *w2, layer=l, tile=xs.shape[0], sub=xs.shape[0], final_norm=last)

    return (xp.reshape(batch, seq, d), xs.reshape(n_streams, n_new, d),
            jnp.stack(kp), jnp.stack(vp), jnp.stack(cp),
            jnp.stack(ks), jnp.stack(vs), jnp.stack(cs))
```

```python
import functools

import jax
import jax.numpy as jnp
from jax import lax
from jax.experimental import pallas as pl
from jax.experimental.pallas import tpu as pltpu

F32 = jnp.float32
BF16 = jnp.bfloat16

CHUNK = 64
N_HEADS = 16
N_KV_HEADS = 2
HEAD_DIM = 64
GROUP = N_HEADS // N_KV_HEADS
PAIRS = GROUP // 2
KV_DIM = N_KV_HEADS * HEAD_DIM
WINDOW = 128
CONV_WIDTH = 3
EPS = 1e-6
ATTN_SCALE = HEAD_DIM ** -0.5

KEY_SPAN = 256
LANES = 128
SUBLANES = 8
VMEM_LIMIT_BYTES = 56 * 1024 * 1024


def _rms(x, g):
    y = x * lax.rsqrt(jnp.mean(x * x, axis=-1, keepdims=True) + EPS)
    return y * g


def _dot(a, b):
    return jnp.dot(a, b, preferred_element_type=F32)


def _dot_nt(a, b):
    return lax.dot_general(a, b, (((1,), (1,)), ((), ())), preferred_element_type=F32)


def _const_spec(shape):
    nd = len(shape)
    return pl.BlockSpec(shape, lambda *_: (0,) * nd, pipeline_mode=pl.Buffered(1))


def _layer_spec(stacked, layer):
    rest = stacked.shape[1:]
    return pl.BlockSpec((None,) + rest, lambda *_: (layer,) + (0,) * len(rest),
                        pipeline_mode=pl.Buffered(1))


def _ffn_rows(x_ref, o_ref, g_ref, wg_ref, wu_ref, wd_ref, gf_ref, final_norm, sub):
    blocks = [pl.ds(r, sub) for r in range(0, x_ref.shape[0], sub)]
    gated = []
    for rows in blocks:
        h = _rms(x_ref[rows, :], g_ref[...]).astype(BF16)
        gated.append((_dot(h, wg_ref[...]), _dot(h, wu_ref[...])))
    for rows, (gate, up) in zip(blocks, gated):
        act = (gate * jax.nn.sigmoid(gate) * up).astype(BF16)
        out = x_ref[rows, :] + 0.5 * _dot(act, wd_ref[...])
        if final_norm:
            out = _rms(out, gf_ref[...])
        o_ref[rows, :] = out


def _ffn_body(xp_ref, xs_ref, g_ref, wg_ref, wu_ref, wd_ref, gf_ref, op_ref, os_ref, *,
              final_norm, sub, n_tiles):
    i = pl.program_id(0)
    params = (g_ref, wg_ref, wu_ref, wd_ref, gf_ref, final_norm)

    @pl.when(i < n_tiles)
    def _():
        _ffn_rows(xp_ref, op_ref, *params, sub)

    @pl.when(i == n_tiles)
    def _():
        _ffn_rows(xs_ref, os_ref, *params, xs_ref.shape[0])


def _ffn(xp, xs, g, wg, wu, wd, gf, *, layer, tile, sub, final_norm):
    n, d = xp.shape
    assert n % tile == 0 and tile % sub == 0
    n_tiles = n // tile

    def tile_index(i):
        return (jnp.minimum(i, n_tiles - 1), 0)

    return pl.pallas_call(
        functools.partial(_ffn_body, final_norm=final_norm, sub=sub, n_tiles=n_tiles),
        grid=(n_tiles + 1,),
        in_specs=[
            pl.BlockSpec((tile, d), tile_index),
            _const_spec(xs.shape),
            _layer_spec(g, layer),
            _layer_spec(wg, layer),
            _layer_spec(wu, layer),
            _layer_spec(wd, layer),
            _const_spec((1, d)),
        ],
        out_specs=[pl.BlockSpec((tile, d), tile_index),
                   pl.BlockSpec(xs.shape, lambda i: (0, 0))],
        out_shape=[jax.ShapeDtypeStruct((n, d), F32), jax.ShapeDtypeStruct(xs.shape, F32)],
        compiler_params=pltpu.CompilerParams(
            dimension_semantics=("arbitrary",), vmem_limit_bytes=VMEM_LIMIT_BYTES),
        name="ffn_final" if final_norm else "ffn",
    )(xp, xs, g, wg, wu, wd, gf)


def _split_offsets(d):
    sizes = (N_HEADS * HEAD_DIM, KV_DIM, KV_DIM, d, d, d, d, d)
    offs = [0]
    for s in sizes:
        offs.append(offs[-1] + s)
    return offs


ONES_ROWS = 16


def _key_placements(k2):
    lo = lax.broadcasted_iota(jnp.int32, k2.shape, 1) < HEAD_DIM
    k2r = pltpu.roll(k2, HEAD_DIM, 1)
    return [jnp.where(lo, k2, 0.0).astype(BF16), jnp.where(lo, k2r, 0.0).astype(BF16),
            jnp.where(lo, 0.0, k2r).astype(BF16), jnp.where(lo, 0.0, k2).astype(BF16)]


def _value_rows(v2):
    v_t = v2.T
    ones = jnp.ones((ONES_ROWS, v2.shape[0]), F32)
    return [jnp.concatenate([v_t[h * HEAD_DIM:(h + 1) * HEAD_DIM, :], ones], axis=0)
            for h in range(N_KV_HEADS)]


def _attn_probs_t(qs, q_row0, nq, keys, sink_ref, bias, n_window):
    qrows = pl.ds(q_row0, nq)
    out = []
    for h in range(N_KV_HEADS):
        base = h * GROUP * HEAD_DIM
        q_stack = jnp.concatenate(
            [qs[qrows, base + p * LANES:base + (p + 1) * LANES] for p in range(PAIRS)], axis=0)
        s_both = _dot_nt(keys[h], q_stack)
        n_keys = keys[h].shape[0] // 2
        per_head = []
        for i in range(2):
            s = s_both[i * n_keys:(i + 1) * n_keys, :]
            if bias is not None:
                s = s + bias
            sink = sink_ref[h, i]
            m = jnp.maximum(jnp.max(s, axis=0, keepdims=True), sink)
            e = jnp.exp(s - m).astype(BF16)
            if n_keys < n_window:
                e = jnp.concatenate([jnp.zeros((n_window - n_keys, e.shape[1]), BF16), e], axis=0)
            per_head.append((e, jnp.exp(sink - m)))
        out.append(per_head)
    return out


def _attn_output_t(at, q_row0, nq, probs, values):
    qrows = pl.ds(q_row0, nq)
    for h in range(N_KV_HEADS):
        base = h * GROUP * HEAD_DIM
        halves = []
        for e, e_sink in probs[h]:
            o_t = _dot(values[h], e)
            halves.append(o_t[0:HEAD_DIM, :] / (o_t[HEAD_DIM:HEAD_DIM + 1, :] + e_sink))
        o = jnp.concatenate(halves, axis=0).T.astype(BF16)
        for p in range(PAIRS):
            at[qrows, base + p * LANES:base + (p + 1) * LANES] = o[p * nq:(p + 1) * nq, :]


def _conv_taps(ub, cw_ref, n):
    out = ub[pl.ds(SUBLANES - 2, n), :] * cw_ref[0:1, :]
    out = out + ub[pl.ds(SUBLANES - 1, n), :] * cw_ref[1:2, :]
    return out + ub[pl.ds(SUBLANES, n), :] * cw_ref[2:3, :]


VISIBLE = WINDOW + CHUNK


def _mixer_prompt_body(x_ref, g_ref, win_ref, bin_ref, sink_ref, cw_ref, woa_ref, woc_ref, wo_ref,
                       xo_ref, kn_ref, vn_ref, un_ref,
                       qs, at, ub, *carry_refs, tile):
    d = x_ref.shape[1]
    offs = _split_offsets(d)
    t = pl.program_id(1)
    kc_refs = carry_refs[0:2 * N_KV_HEADS]
    vc_refs = carry_refs[2 * N_KV_HEADS:]
    cur = lax.rem(t, 2)
    nxt = 1 - cur

    @pl.when(t == 0)
    def _():
        for r in carry_refs:
            r[0] = jnp.zeros(r.shape[1:], r.dtype)
        ub[0, 0:SUBLANES, :] = jnp.zeros((SUBLANES, d), F32)

    x = x_ref[...]
    h = _rms(x, g_ref[...]).astype(BF16)

    def proj(i, n=1):
        return _dot(h, win_ref[:, offs[i]:offs[i + n]]) + bin_ref[:, offs[i]:offs[i + n]]

    def proj_kv():
        kv = proj(1, 2)
        return kv[:, 0:KV_DIM], kv[:, KV_DIM:2 * KV_DIM]

    k2, v2 = proj_kv()
    kn_ref[0] = k2[tile - WINDOW:, :]
    vn_ref[0] = v2[tile - WINDOW:, :]
    qs[...] = (proj(0) * ATTN_SCALE).astype(BF16)
    c_gate = proj(4)

    k_placed = _key_placements(k2)
    k_carried = [r[cur] for r in kc_refs]

    def key_window(i, r0):
        if r0 >= WINDOW:
            return k_placed[i][r0 - WINDOW:r0 - WINDOW + VISIBLE, :]
        return jnp.concatenate([k_carried[i][r0:, :], k_placed[i][0:r0 + CHUNK, :]], axis=0)

    width = KEY_SPAN + tile
    v_bufs = []
    for hh, rows in enumerate(_value_rows(v2)):
        buf_a = jnp.concatenate([vc_refs[hh][cur], rows], axis=1)
        buf_b = pltpu.roll(buf_a, width - CHUNK, 1)
        v_bufs.append((buf_a.astype(BF16), buf_b.astype(BF16)))
        vc_refs[hh][nxt] = buf_a[:, tile:tile + KEY_SPAN]

    def value_window(hh, c):
        buf_a, buf_b = v_bufs[hh]
        if c % 2:
            return buf_b[:, (c + 1) * CHUNK:(c + 1) * CHUNK + VISIBLE]
        return buf_a[:, (c + 2) * CHUNK:(c + 2) * CHUNK + VISIBLE]

    j = lax.broadcasted_iota(jnp.int32, (VISIBLE, PAIRS * CHUNK), 0)
    probs = []
    for c in range(tile // CHUNK):
        r0 = c * CHUNK
        pos = t * tile + (r0 - WINDOW) + j
        bias = jnp.where(pos >= 0, 0.0, -jnp.inf).astype(F32)
        keys = [jnp.concatenate([key_window(hh, r0), key_window(N_KV_HEADS + hh, r0)], axis=0)
                for hh in range(N_KV_HEADS)]
        probs.append(_attn_probs_t(qs, r0, CHUNK, keys, sink_ref, bias, VISIBLE))

    u = c_gate * proj(5)
    ub[cur, SUBLANES:SUBLANES + tile, :] = u
    conv = (proj(3) * _conv_taps(ub.at[cur], cw_ref, tile)).astype(BF16)
    un_ref[0] = u[tile - SUBLANES:, :]
    ub[nxt, 0:SUBLANES, :] = u[tile - SUBLANES:, :]
    gate_attn = jax.nn.sigmoid(proj(6))
    gate_conv = jax.nn.sigmoid(proj(7))

    for c in range(tile // CHUNK):
        values = [value_window(hh, c) for hh in range(N_KV_HEADS)]
        _attn_output_t(at, c * CHUNK, CHUNK, probs[c], values)

    y_conv = gate_conv * _dot(conv, woc_ref[...])
    merged = gate_attn * _dot(at[...], woa_ref[...]) + y_conv
    xo_ref[...] = x + _dot(merged.astype(BF16), wo_ref[...])

    for r, a in zip(kc_refs, k_placed):
        r[nxt] = a[tile - WINDOW:, :]


def _mixer_prompt(x2, g, w_in, b_in, sink_cols, conv_w, woa, woc, wo, *, layer, batch, tile):
    n, d = x2.shape
    seq = n // batch
    assert seq % tile == 0 and tile % (2 * CHUNK) == 0 and tile >= KEY_SPAN
    nt = seq // tile
    return pl.pallas_call(
        functools.partial(_mixer_prompt_body, tile=tile),
        grid=(batch, nt),
        in_specs=[
            pl.BlockSpec((tile, d), lambda b, t: (b * nt + t, 0)),
            _layer_spec(g, layer),
            _layer_spec(w_in, layer),
            _layer_spec(b_in, layer),
            _const_spec(sink_cols.shape),
            _layer_spec(conv_w, layer),
            _layer_spec(woa, layer),
            _layer_spec(woc, layer),
            _layer_spec(wo, layer),
        ],
        out_specs=[
            pl.BlockSpec((tile, d), lambda b, t: (b * nt + t, 0)),
            pl.BlockSpec((1, WINDOW, KV_DIM), lambda b, t: (b, 0, 0)),
            pl.BlockSpec((1, WINDOW, KV_DIM), lambda b, t: (b, 0, 0)),
            pl.BlockSpec((1, SUBLANES, d), lambda b, t: (b, 0, 0)),
        ],
        out_shape=[
            jax.ShapeDtypeStruct((n, d), F32),
            jax.ShapeDtypeStruct((batch, WINDOW, KV_DIM), F32),
            jax.ShapeDtypeStruct((batch, WINDOW, KV_DIM), F32),
            jax.ShapeDtypeStruct((batch, SUBLANES, d), F32),
        ],
        scratch_shapes=[
            pltpu.VMEM((tile, d), BF16),
            pltpu.VMEM((tile, d), BF16),
            pltpu.VMEM((2, SUBLANES + tile, d), F32),
        ] + [pltpu.VMEM((2, WINDOW, LANES), BF16) for _ in range(2 * N_KV_HEADS)]
          + [pltpu.VMEM((2, HEAD_DIM + ONES_ROWS, KEY_SPAN), F32) for _ in range(N_KV_HEADS)],
        compiler_params=pltpu.CompilerParams(
            dimension_semantics=("arbitrary", "arbitrary"), vmem_limit_bytes=VMEM_LIMIT_BYTES),
        name="mixer_prompt",
    )(x2, g, w_in, b_in, sink_cols, conv_w, woa, woc, wo)


def _mixer_sample_body(x_ref, ck_ref, cv_ref, st_ref, g_ref, win_ref, bin_ref, sink_ref, cw_ref,
                       woa_ref, woc_ref, wo_ref,
                       xo_ref, kn_ref, vn_ref, un_ref,
                       qs, at, cs, ub, *, n_streams, n_new):
    d = x_ref.shape[1]
    offs = _split_offsets(d)
    keep = ck_ref.shape[1]
    x = x_ref[...]
    h = _rms(x, g_ref[...]).astype(BF16)

    def proj(i, n=1):
        return _dot(h, win_ref[:, offs[i]:offs[i + n]]) + bin_ref[:, offs[i]:offs[i + n]]

    def proj_kv():
        kv = proj(1, 2)
        return kv[:, 0:KV_DIM], kv[:, KV_DIM:2 * KV_DIM]

    kn_all, vn_all = proj_kv()
    qs[...] = (proj(0) * ATTN_SCALE).astype(BF16)
    u_all = proj(4) * proj(5)

    pad = jnp.zeros((KEY_SPAN - keep - n_new, KV_DIM), F32)

    for b in range(n_streams):
        r0 = b * n_new
        k_new = kn_all[r0:r0 + n_new, :]
        v_new = vn_all[r0:r0 + n_new, :]
        kn_ref[b, 0:keep - n_new, :] = ck_ref[b, n_new:keep, :]
        kn_ref[b, keep - n_new:keep, :] = k_new
        vn_ref[b, 0:keep - n_new, :] = cv_ref[b, n_new:keep, :]
        vn_ref[b, keep - n_new:keep, :] = v_new
        placed = _key_placements(jnp.concatenate([ck_ref[b], k_new], axis=0))
        keys = [jnp.concatenate([placed[hh], placed[N_KV_HEADS + hh]], axis=0)
                for hh in range(N_KV_HEADS)]
        values = [rows.astype(BF16)
                  for rows in _value_rows(jnp.concatenate([pad, cv_ref[b], v_new], axis=0))]
        probs = _attn_probs_t(qs, r0, n_new, keys, sink_ref, None, KEY_SPAN)
        _attn_output_t(at, r0, n_new, probs, values)

        ub[b, 0:SUBLANES, :] = st_ref[b]
        ub[b, SUBLANES:SUBLANES + n_new, :] = u_all[r0:r0 + n_new, :]
        cs[r0:r0 + n_new, :] = _conv_taps(ub.at[b], cw_ref, n_new)
        un_ref[b] = u_all[r0 + n_new - SUBLANES:r0 + n_new, :]

    conv = (proj(3) * cs[...]).astype(BF16)
    merged = (jax.nn.sigmoid(proj(6)) * _dot(at[...], woa_ref[...])
              + jax.nn.sigmoid(proj(7)) * _dot(conv, woc_ref[...]))
    xo_ref[...] = x + _dot(merged.astype(BF16), wo_ref[...])


def _mixer_sample(x2, ck, cv, st, g, w_in, b_in, sink_cols, conv_w, woa, woc, wo, *, layer):
    n, d = x2.shape
    _, n_streams, keep, _ = ck.shape
    n_new = n // n_streams
    assert keep + n_new <= KEY_SPAN and n_new % SUBLANES == 0 and keep >= n_new
    args = (x2, ck, cv, st, g, w_in, b_in, sink_cols, conv_w, woa, woc, wo)

    def whole(a):
        nd = len(a.shape)
        return pl.BlockSpec(a.shape, lambda i: (0,) * nd)

    out_shape = [
        jax.ShapeDtypeStruct((n, d), F32),
        jax.ShapeDtypeStruct((n_streams, keep, KV_DIM), F32),
        jax.ShapeDtypeStruct((n_streams, keep, KV_DIM), F32),
        jax.ShapeDtypeStruct((n_streams, SUBLANES, d), F32),
    ]
    return pl.pallas_call(
        functools.partial(_mixer_sample_body, n_streams=n_streams, n_new=n_new),
        grid=(1,),
        in_specs=[whole(a) if a is x2 or a is sink_cols else _layer_spec(a, layer) for a in args],
        out_specs=[whole(s) for s in out_shape],
        out_shape=out_shape,
        scratch_shapes=[
            pltpu.VMEM((n, d), BF16),
            pltpu.VMEM((n, d), BF16),
            pltpu.VMEM((n, d), F32),
            pltpu.VMEM((n_streams, SUBLANES + n_new, d), F32),
        ],
        compiler_params=pltpu.CompilerParams(
            dimension_semantics=("arbitrary",), vmem_limit_bytes=VMEM_LIMIT_BYTES),
        name="mixer_sample",
    )(*args)


FFN_SUB_ROWS = 512
FFN_TILE_ROWS = 1024
MIXER_TILE_ROWS = 512


def _tile_sizes(n_rows, seq):
    sub = FFN_SUB_ROWS if n_rows % FFN_SUB_ROWS == 0 else n_rows
    tile = FFN_TILE_ROWS if n_rows % FFN_TILE_ROWS == 0 else sub
    mix = MIXER_TILE_ROWS if seq % MIXER_TILE_ROWS == 0 else seq
    return tile, sub, mix


def _sink_rows(sinks_l, nq):
    s = sinks_l.astype(F32).reshape(N_KV_HEADS, PAIRS, 2)
    s = jnp.transpose(s, (0, 2, 1))
    return jnp.repeat(s, nq, axis=2)[:, :, None, :]


def kernel(x_prompt, x_sample, cache_k, cache_v, state_conv, norm_ffn1, w1_gate, w1_up, w1_down,
           norm_mix, w_in, b_in, sinks, conv_w, w_o_attn, w_o_conv, w_o, norm_ffn2, w2_gate,
           w2_up, w2_down, norm_final):
    batch, seq, d = x_prompt.shape
    n_streams, n_new, _ = x_sample.shape
    depth = w_in.shape[0]
    keep_s = cache_k.shape[2]
    tile_ffn, sub_ffn, tile_mix = _tile_sizes(batch * seq, seq)

    xp = x_prompt.reshape(batch * seq, d)
    xs = x_sample.reshape(n_streams * n_new, d)
    gf = norm_final.reshape(1, d)
    st_pad = jnp.pad(state_conv, ((0, 0), (0, 0), (SUBLANES - (CONV_WIDTH - 1), 0), (0, 0)))
    ck = cache_k.reshape(depth, n_streams, keep_s, KV_DIM)
    cv = cache_v.reshape(depth, n_streams, keep_s, KV_DIM)

    w1 = (norm_ffn1.reshape(depth, 1, d), w1_gate.astype(BF16), w1_up.astype(BF16),
          w1_down.astype(BF16), gf)
    w2 = (norm_ffn2.reshape(depth, 1, d), w2_gate.astype(BF16), w2_up.astype(BF16),
          w2_down.astype(BF16), gf)
    wm = (norm_mix.reshape(depth, 1, d), w_in.astype(BF16), b_in.reshape(depth, 1, -1))
    wo = (conv_w, w_o_attn.astype(BF16), w_o_conv.astype(BF16), w_o.astype(BF16))

    kp, vp, cp, ks, vs, cs = [], [], [], [], [], []
    for l in range(depth):
        last = l == depth - 1
        xp, xs = _ffn(xp, xs, *w1, layer=l, tile=tile_ffn, sub=sub_ffn, final_norm=False)

        xp, k_l, v_l, u_l = _mixer_prompt(
            xp, *wm, _sink_rows(sinks[l], CHUNK), *wo, layer=l, batch=batch, tile=tile_mix)
        kp.append(k_l.reshape(batch, WINDOW, N_KV_HEADS, HEAD_DIM))
        vp.append(v_l.reshape(batch, WINDOW, N_KV_HEADS, HEAD_DIM))
        cp.append(u_l[:, SUBLANES - (CONV_WIDTH - 1):, :])

        xs, k_l, v_l, u_l = _mixer_sample(
            xs, ck, cv, st_pad, *wm, _sink_rows(sinks[l], n_new), *wo, layer=l)
        ks.append(k_l.reshape(n_streams, keep_s, N_KV_HEADS, HEAD_DIM))
        vs.append(v_l.reshape(n_streams, keep_s, N_KV_HEADS, HEAD_DIM))
        cs.append(u_l[:, SUBLANES - (CONV_WIDTH - 1):, :])

        xp, xs = _ffn(xp, xs, *w2, layer=l, tile=tile_ffn, sub=sub_ffn, final_norm=last)

    return (xp.reshape(batch, seq, d), xs.reshape(n_streams, n_new, d),
            jnp.stack(kp), jnp.stack(vp), jnp.stack(cp),
            jnp.stack(ks), jnp.stack(vs), jnp.stack(cs))
```

```python
import functools

import jax
import jax.numpy as jnp
from jax import lax
from jax.experimental import pallas as pl
from jax.experimental.pallas import tpu as pltpu

F32 = jnp.float32
BF16 = jnp.bfloat16

CHUNK = 64
N_HEADS = 16
N_KV_HEADS = 2
HEAD_DIM = 64
GROUP = N_HEADS // N_KV_HEADS
PAIRS = GROUP // 2
KV_DIM = N_KV_HEADS * HEAD_DIM
WINDOW = 128
CONV_WIDTH = 3
EPS = 1e-6
ATTN_SCALE = HEAD_DIM ** -0.5

KEY_SPAN = 256
LANES = 128
SUBLANES = 8
VMEM_LIMIT_BYTES = 56 * 1024 * 1024


def _rms(x, g):
    y = x * lax.rsqrt(jnp.mean(x * x, axis=-1, keepdims=True) + EPS)
    return y * g


def _dot(a, b):
    return jnp.dot(a, b, preferred_element_type=F32)


def _dot_nt(a, b):
    return lax.dot_general(a, b, (((1,), (1,)), ((), ())), preferred_element_type=F32)


def _const_spec(shape):
    nd = len(shape)
    return pl.BlockSpec(shape, lambda *_: (0,) * nd, pipeline_mode=pl.Buffered(1))


def _layer_spec(stacked, layer):
    rest = stacked.shape[1:]
    return pl.BlockSpec((None,) + rest, lambda *_: (layer,) + (0,) * len(rest),
                        pipeline_mode=pl.Buffered(1))


def _ffn_body(x_ref, g_ref, wg_ref, wu_ref, wd_ref, gf_ref, o_ref, *, final_norm, sub):
    blocks = [pl.ds(r, sub) for r in range(0, x_ref.shape[0], sub)]
    gated = []
    for rows in blocks:
        h = _rms(x_ref[rows, :], g_ref[...]).astype(BF16)
        gated.append((_dot(h, wg_ref[...]), _dot(h, wu_ref[...])))
    for rows, (gate, up) in zip(blocks, gated):
        act = (gate * jax.nn.sigmoid(gate) * up).astype(BF16)
        out = x_ref[rows, :] + 0.5 * _dot(act, wd_ref[...])
        if final_norm:
            out = _rms(out, gf_ref[...])
        o_ref[rows, :] = out


def _ffn(x2, g, wg, wu, wd, gf, *, layer, tile, sub, final_norm):
    n, d = x2.shape
    assert n % tile == 0 and tile % sub == 0
    return pl.pallas_call(
        functools.partial(_ffn_body, final_norm=final_norm, sub=sub),
        grid=(n // tile,),
        in_specs=[
            pl.BlockSpec((tile, d), lambda i: (i, 0)),
            _layer_spec(g, layer),
            _layer_spec(wg, layer),
            _layer_spec(wu, layer),
            _layer_spec(wd, layer),
            _const_spec((1, d)),
        ],
        out_specs=pl.BlockSpec((tile, d), lambda i: (i, 0)),
        out_shape=jax.ShapeDtypeStruct((n, d), F32),
        compiler_params=pltpu.CompilerParams(
            dimension_semantics=("arbitrary",), vmem_limit_bytes=VMEM_LIMIT_BYTES),
        name="ffn_final" if final_norm else "ffn",
    )(x2, g, wg, wu, wd, gf)


def _split_offsets(d):
    sizes = (N_HEADS * HEAD_DIM, KV_DIM, KV_DIM, d, d, d, d, d)
    offs = [0]
    for s in sizes:
        offs.append(offs[-1] + s)
    return offs


ONES_ROWS = 16


def _key_placements(k2):
    lo = lax.broadcasted_iota(jnp.int32, k2.shape, 1) < HEAD_DIM
    k2r = pltpu.roll(k2, HEAD_DIM, 1)
    return [jnp.where(lo, k2, 0.0).astype(BF16), jnp.where(lo, k2r, 0.0).astype(BF16),
            jnp.where(lo, 0.0, k2r).astype(BF16), jnp.where(lo, 0.0, k2).astype(BF16)]


def _value_rows(v2):
    v_t = v2.T
    ones = jnp.ones((ONES_ROWS, v2.shape[0]), F32)
    return [jnp.concatenate([v_t[h * HEAD_DIM:(h + 1) * HEAD_DIM, :], ones], axis=0)
            for h in range(N_KV_HEADS)]


def _attn_probs_t(qs, q_row0, nq, keys, sink_ref, bias, n_window):
    qrows = pl.ds(q_row0, nq)
    out = []
    for h in range(N_KV_HEADS):
        base = h * GROUP * HEAD_DIM
        q_stack = jnp.concatenate(
            [qs[qrows, base + p * LANES:base + (p + 1) * LANES] for p in range(PAIRS)], axis=0)
        s_both = _dot_nt(keys[h], q_stack)
        n_keys = keys[h].shape[0] // 2
        per_head = []
        for i in range(2):
            s = s_both[i * n_keys:(i + 1) * n_keys, :]
            if bias is not None:
                s = s + bias
            sink = sink_ref[h, i]
            m = jnp.maximum(jnp.max(s, axis=0, keepdims=True), sink)
            e = jnp.exp(s - m).astype(BF16)
            if n_keys < n_window:
                e = jnp.concatenate([jnp.zeros((n_window - n_keys, e.shape[1]), BF16), e], axis=0)
            per_head.append((e, jnp.exp(sink - m)))
        out.append(per_head)
    return out


def _attn_output_t(at, q_row0, nq, probs, values):
    qrows = pl.ds(q_row0, nq)
    for h in range(N_KV_HEADS):
        base = h * GROUP * HEAD_DIM
        halves = []
        for e, e_sink in probs[h]:
            o_t = _dot(values[h], e)
            halves.append(o_t[0:HEAD_DIM, :] / (o_t[HEAD_DIM:HEAD_DIM + 1, :] + e_sink))
        o = jnp.concatenate(halves, axis=0).T.astype(BF16)
        for p in range(PAIRS):
            at[qrows, base + p * LANES:base + (p + 1) * LANES] = o[p * nq:(p + 1) * nq, :]


def _conv_taps(ub, cw_ref, n):
    out = ub[pl.ds(SUBLANES - 2, n), :] * cw_ref[0:1, :]
    out = out + ub[pl.ds(SUBLANES - 1, n), :] * cw_ref[1:2, :]
    return out + ub[pl.ds(SUBLANES, n), :] * cw_ref[2:3, :]


VISIBLE = WINDOW + CHUNK


def _mixer_prompt_body(x_ref, g_ref, win_ref, bin_ref, sink_ref, cw_ref, woa_ref, woc_ref, wo_ref,
                       xo_ref, kn_ref, vn_ref, un_ref,
                       qs, at, ub, *carry_refs, tile):
    d = x_ref.shape[1]
    offs = _split_offsets(d)
    t = pl.program_id(1)
    kc_refs = carry_refs[0:2 * N_KV_HEADS]
    vc_refs = carry_refs[2 * N_KV_HEADS:]
    cur = lax.rem(t, 2)
    nxt = 1 - cur

    @pl.when(t == 0)
    def _():
        for r in carry_refs:
            r[0] = jnp.zeros(r.shape[1:], r.dtype)
        ub[0, 0:SUBLANES, :] = jnp.zeros((SUBLANES, d), F32)

    x = x_ref[...]
    h = _rms(x, g_ref[...]).astype(BF16)

    def proj(i, n=1):
        return _dot(h, win_ref[:, offs[i]:offs[i + n]]) + bin_ref[:, offs[i]:offs[i + n]]

    def proj_kv():
        kv = proj(1, 2)
        return kv[:, 0:KV_DIM], kv[:, KV_DIM:2 * KV_DIM]

    k2, v2 = proj_kv()
    kn_ref[0] = k2[tile - WINDOW:, :]
    vn_ref[0] = v2[tile - WINDOW:, :]
    qs[...] = (proj(0) * ATTN_SCALE).astype(BF16)
    c_gate = proj(4)

    k_placed = _key_placements(k2)
    k_carried = [r[cur] for r in kc_refs]

    def key_window(i, r0):
        if r0 >= WINDOW:
            return k_placed[i][r0 - WINDOW:r0 - WINDOW + VISIBLE, :]
        return jnp.concatenate([k_carried[i][r0:, :], k_placed[i][0:r0 + CHUNK, :]], axis=0)

    width = KEY_SPAN + tile
    v_bufs = []
    for hh, rows in enumerate(_value_rows(v2)):
        buf_a = jnp.concatenate([vc_refs[hh][cur], rows], axis=1)
        buf_b = pltpu.roll(buf_a, width - CHUNK, 1)
        v_bufs.append((buf_a.astype(BF16), buf_b.astype(BF16)))
        vc_refs[hh][nxt] = buf_a[:, tile:tile + KEY_SPAN]

    def value_window(hh, c):
        buf_a, buf_b = v_bufs[hh]
        if c % 2:
            return buf_b[:, (c + 1) * CHUNK:(c + 1) * CHUNK + VISIBLE]
        return buf_a[:, (c + 2) * CHUNK:(c + 2) * CHUNK + VISIBLE]

    j = lax.broadcasted_iota(jnp.int32, (VISIBLE, PAIRS * CHUNK), 0)
    probs = []
    for c in range(tile // CHUNK):
        r0 = c * CHUNK
        bias = None
        if r0 < WINDOW:
            pos = t * tile + (r0 - WINDOW) + j
            bias = jnp.where(pos >= 0, 0.0, -jnp.inf).astype(F32)
        keys = [jnp.concatenate([key_window(hh, r0), key_window(N_KV_HEADS + hh, r0)], axis=0)
                for hh in range(N_KV_HEADS)]
        probs.append(_attn_probs_t(qs, r0, CHUNK, keys, sink_ref, bias, VISIBLE))

    u = c_gate * proj(5)
    ub[cur, SUBLANES:SUBLANES + tile, :] = u
    conv = (proj(3) * _conv_taps(ub.at[cur], cw_ref, tile)).astype(BF16)
    un_ref[0] = u[tile - SUBLANES:, :]
    ub[nxt, 0:SUBLANES, :] = u[tile - SUBLANES:, :]
    gate_attn = jax.nn.sigmoid(proj(6))
    gate_conv = jax.nn.sigmoid(proj(7))

    for c in range(tile // CHUNK):
        values = [value_window(hh, c) for hh in range(N_KV_HEADS)]
        _attn_output_t(at, c * CHUNK, CHUNK, probs[c], values)

    y_conv = gate_conv * _dot(conv, woc_ref[...])
    merged = gate_attn * _dot(at[...], woa_ref[...]) + y_conv
    xo_ref[...] = x + _dot(merged.astype(BF16), wo_ref[...])

    for r, a in zip(kc_refs, k_placed):
        r[nxt] = a[tile - WINDOW:, :]


def _mixer_prompt(x2, g, w_in, b_in, sink_cols, conv_w, woa, woc, wo, *, layer, batch, tile):
    n, d = x2.shape
    seq = n // batch
    assert seq % tile == 0 and tile % (2 * CHUNK) == 0 and tile >= KEY_SPAN
    nt = seq // tile
    return pl.pallas_call(
        functools.partial(_mixer_prompt_body, tile=tile),
        grid=(batch, nt),
        in_specs=[
            pl.BlockSpec((tile, d), lambda b, t: (b * nt + t, 0)),
            _layer_spec(g, layer),
            _layer_spec(w_in, layer),
            _layer_spec(b_in, layer),
            _const_spec(sink_cols.shape),
            _layer_spec(conv_w, layer),
            _layer_spec(woa, layer),
            _layer_spec(woc, layer),
            _layer_spec(wo, layer),
        ],
        out_specs=[
            pl.BlockSpec((tile, d), lambda b, t: (b * nt + t, 0)),
            pl.BlockSpec((1, WINDOW, KV_DIM), lambda b, t: (b, 0, 0)),
            pl.BlockSpec((1, WINDOW, KV_DIM), lambda b, t: (b, 0, 0)),
            pl.BlockSpec((1, SUBLANES, d), lambda b, t: (b, 0, 0)),
        ],
        out_shape=[
            jax.ShapeDtypeStruct((n, d), F32),
            jax.ShapeDtypeStruct((batch, WINDOW, KV_DIM), F32),
            jax.ShapeDtypeStruct((batch, WINDOW, KV_DIM), F32),
            jax.ShapeDtypeStruct((batch, SUBLANES, d), F32),
        ],
        scratch_shapes=[
            pltpu.VMEM((tile, d), BF16),
            pltpu.VMEM((tile, d), BF16),
            pltpu.VMEM((2, SUBLANES + tile, d), F32),
        ] + [pltpu.VMEM((2, WINDOW, LANES), BF16) for _ in range(2 * N_KV_HEADS)]
          + [pltpu.VMEM((2, HEAD_DIM + ONES_ROWS, KEY_SPAN), F32) for _ in range(N_KV_HEADS)],
        compiler_params=pltpu.CompilerParams(
            dimension_semantics=("arbitrary", "arbitrary"), vmem_limit_bytes=VMEM_LIMIT_BYTES),
        name="mixer_prompt",
    )(x2, g, w_in, b_in, sink_cols, conv_w, woa, woc, wo)


def _mixer_sample_body(x_ref, ck_ref, cv_ref, st_ref, g_ref, win_ref, bin_ref, sink_ref, cw_ref,
                       woa_ref, woc_ref, wo_ref,
                       xo_ref, kn_ref, vn_ref, un_ref,
                       qs, at, cs, ub, *, n_streams, n_new):
    d = x_ref.shape[1]
    offs = _split_offsets(d)
    keep = ck_ref.shape[1]
    x = x_ref[...]
    h = _rms(x, g_ref[...]).astype(BF16)

    def proj(i, n=1):
        return _dot(h, win_ref[:, offs[i]:offs[i + n]]) + bin_ref[:, offs[i]:offs[i + n]]

    def proj_kv():
        kv = proj(1, 2)
        return kv[:, 0:KV_DIM], kv[:, KV_DIM:2 * KV_DIM]

    kn_all, vn_all = proj_kv()
    qs[...] = (proj(0) * ATTN_SCALE).astype(BF16)
    u_all = proj(4) * proj(5)

    pad = jnp.zeros((KEY_SPAN - keep - n_new, KV_DIM), F32)

    for b in range(n_streams):
        r0 = b * n_new
        k_new = kn_all[r0:r0 + n_new, :]
        v_new = vn_all[r0:r0 + n_new, :]
        kn_ref[b, 0:keep - n_new, :] = ck_ref[b, n_new:keep, :]
        kn_ref[b, keep - n_new:keep, :] = k_new
        vn_ref[b, 0:keep - n_new, :] = cv_ref[b, n_new:keep, :]
        vn_ref[b, keep - n_new:keep, :] = v_new
        placed = _key_placements(jnp.concatenate([ck_ref[b], k_new], axis=0))
        keys = [jnp.concatenate([placed[hh], placed[N_KV_HEADS + hh]], axis=0)
                for hh in range(N_KV_HEADS)]
        values = [rows.astype(BF16)
                  for rows in _value_rows(jnp.concatenate([pad, cv_ref[b], v_new], axis=0))]
        probs = _attn_probs_t(qs, r0, n_new, keys, sink_ref, None, KEY_SPAN)
        _attn_output_t(at, r0, n_new, probs, values)

        ub[b, 0:SUBLANES, :] = st_ref[b]
        ub[b, SUBLANES:SUBLANES + n_new, :] = u_all[r0:r0 + n_new, :]
        cs[r0:r0 + n_new, :] = _conv_taps(ub.at[b], cw_ref, n_new)
        un_ref[b] = u_all[r0 + n_new - SUBLANES:r0 + n_new, :]

    conv = (proj(3) * cs[...]).astype(BF16)
    merged = (jax.nn.sigmoid(proj(6)) * _dot(at[...], woa_ref[...])
              + jax.nn.sigmoid(proj(7)) * _dot(conv, woc_ref[...]))
    xo_ref[...] = x + _dot(merged.astype(BF16), wo_ref[...])


def _mixer_sample(x2, ck, cv, st, g, w_in, b_in, sink_cols, conv_w, woa, woc, wo, *, layer):
    n, d = x2.shape
    _, n_streams, keep, _ = ck.shape
    n_new = n // n_streams
    assert keep + n_new <= KEY_SPAN and n_new % SUBLANES == 0 and keep >= n_new
    args = (x2, ck, cv, st, g, w_in, b_in, sink_cols, conv_w, woa, woc, wo)

    def whole(a):
        nd = len(a.shape)
        return pl.BlockSpec(a.shape, lambda i: (0,) * nd)

    out_shape = [
        jax.ShapeDtypeStruct((n, d), F32),
        jax.ShapeDtypeStruct((n_streams, keep, KV_DIM), F32),
        jax.ShapeDtypeStruct((n_streams, keep, KV_DIM), F32),
        jax.ShapeDtypeStruct((n_streams, SUBLANES, d), F32),
    ]
    return pl.pallas_call(
        functools.partial(_mixer_sample_body, n_streams=n_streams, n_new=n_new),
        grid=(1,),
        in_specs=[whole(a) if a is x2 or a is sink_cols else _layer_spec(a, layer) for a in args],
        out_specs=[whole(s) for s in out_shape],
        out_shape=out_shape,
        scratch_shapes=[
            pltpu.VMEM((n, d), BF16),
            pltpu.VMEM((n, d), BF16),
            pltpu.VMEM((n, d), F32),
            pltpu.VMEM((n_streams, SUBLANES + n_new, d), F32),
        ],
        compiler_params=pltpu.CompilerParams(
            dimension_semantics=("arbitrary",), vmem_limit_bytes=VMEM_LIMIT_BYTES),
        name="mixer_sample",
    )(*args)


FFN_SUB_ROWS = 512
FFN_TILE_ROWS = 1024
MIXER_TILE_ROWS = 512


def _tile_sizes(n_rows, seq):
    sub = FFN_SUB_ROWS if n_rows % FFN_SUB_ROWS == 0 else n_rows
    tile = FFN_TILE_ROWS if n_rows % FFN_TILE_ROWS == 0 else sub
    mix = MIXER_TILE_ROWS if seq % MIXER_TILE_ROWS == 0 else seq
    return tile, sub, mix


def _sink_rows(sinks_l, nq):
    s = sinks_l.astype(F32).reshape(N_KV_HEADS, PAIRS, 2)
    s = jnp.transpose(s, (0, 2, 1))
    return jnp.repeat(s, nq, axis=2)[:, :, None, :]


def kernel(x_prompt, x_sample, cache_k, cache_v, state_conv, norm_ffn1, w1_gate, w1_up, w1_down,
           norm_mix, w_in, b_in, sinks, conv_w, w_o_attn, w_o_conv, w_o, norm_ffn2, w2_gate,
           w2_up, w2_down, norm_final):
    batch, seq, d = x_prompt.shape
    n_streams, n_new, _ = x_sample.shape
    depth = w_in.shape[0]
    keep_s = cache_k.shape[2]
    tile_ffn, sub_ffn, tile_mix = _tile_sizes(batch * seq, seq)

    xp = x_prompt.reshape(batch * seq, d)
    xs = x_sample.reshape(n_streams * n_new, d)
    gf = norm_final.reshape(1, d)
    st_pad = jnp.pad(state_conv, ((0, 0), (0, 0), (SUBLANES - (CONV_WIDTH - 1), 0), (0, 0)))
    ck = cache_k.reshape(depth, n_streams, keep_s, KV_DIM)
    cv = cache_v.reshape(depth, n_streams, keep_s, KV_DIM)

    w1 = (norm_ffn1.reshape(depth, 1, d), w1_gate.astype(BF16), w1_up.astype(BF16),
          w1_down.astype(BF16), gf)
    w2 = (norm_ffn2.reshape(depth, 1, d), w2_gate.astype(BF16), w2_up.astype(BF16),
          w2_down.astype(BF16), gf)
    wm = (norm_mix.reshape(depth, 1, d), w_in.astype(BF16), b_in.reshape(depth, 1, -1))
    wo = (conv_w, w_o_attn.astype(BF16), w_o_conv.astype(BF16), w_o.astype(BF16))

    kp, vp, cp, ks, vs, cs = [], [], [], [], [], []
    for l in range(depth):
        last = l == depth - 1
        xp = _ffn(xp, *w1, layer=l, tile=tile_ffn, sub=sub_ffn, final_norm=False)
        xs = _ffn(xs, *w1, layer=l, tile=xs.shape[0], sub=xs.shape[0], final_norm=False)

        xp, k_l, v_l, u_l = _mixer_prompt(
            xp, *wm, _sink_rows(sinks[l], CHUNK), *wo, layer=l, batch=batch, tile=tile_mix)
        kp.append(k_l.reshape(batch, WINDOW, N_KV_HEADS, HEAD_DIM))
        vp.append(v_l.reshape(batch, WINDOW, N_KV_HEADS, HEAD_DIM))
        cp.append(u_l[:, SUBLANES - (CONV_WIDTH - 1):, :])

        xs, k_l, v_l, u_l = _mixer_sample(
            xs, ck, cv, st_pad, *wm, _sink_rows(sinks[l], n_new), *wo, layer=l)
        ks.append(k_l.reshape(n_streams, keep_s, N_KV_HEADS, HEAD_DIM))
        vs.append(v_l.reshape(n_streams, keep_s, N_KV_HEADS, HEAD_DIM))
        cs.append(u_l[:, SUBLANES - (CONV_WIDTH - 1):, :])

        xp = _ffn(xp, *w2, layer=l, tile=tile_ffn, sub=sub_ffn, final_norm=last)
        xs = _ffn(xs, *w2, layer=l, tile=xs.shape[0], sub=xs.shape[0], final_norm=last)

    return (xp.reshape(batch, seq, d), xs.reshape(n_streams, n_new, d),
            jnp.stack(kp), jnp.stack(vp), jnp.stack(cp),
            jnp.stack(ks), jnp.stack(vs), jnp.stack(cs))
```
